```python
import math
import jax
import jax.numpy as jnp
from jax import lax
import numpy as np

D_MODEL = 2048
BATCH = 2
SEQ = 4096
DEPTH = 2
DEC_BATCH = 32
DEC_SEQ = 8
PAST_LEN = 8192
PAGE_SIZE = 128

F32 = jnp.float32
EPS = 1e-6
NEG_INF = -1e30
W_SSM = D_MODEL // 4
SSM_GROUP = 16
N_SSM_GROUPS = W_SSM // SSM_GROUP
SSM_STATE = 64
W_POOL = D_MODEL // 4
POOL_WINDOWS = (2, 4, 8, 16)
N_POOL_GROUPS = len(POOL_WINDOWS)
POOL_GROUP = W_POOL // N_POOL_GROUPS
POOL_BUF = max(POOL_WINDOWS) - 1
N_HEADS_ATT = 8
HEAD_DIM_ATT = D_MODEL // (4 * N_HEADS_ATT)
W_ATT = N_HEADS_ATT * 2 * HEAD_DIM_ATT
Q_BLOCK = 128
IN_COLS = W_SSM + W_POOL + 3 * W_ATT + 3 * D_MODEL
N_MEM = 256
N_HEADS_MEM = 4
HEAD_DIM_MEM = D_MODEL // N_HEADS_MEM
N_KEYS = 128
N_EXPERTS = N_KEYS * N_KEYS
N_RETR_HEADS = 8
D_QUERY = 256
TOPK_HALF = 16
TOPK = 16
TOKEN_BLOCK = 128

kernel_name = 'hybrid_s5_pool_diffattn_peer_step'


def _rmsnorm(x, g):
    xf = x.astype(F32)
    y = xf * lax.rsqrt(jnp.mean(xf * xf, axis=-1, keepdims=True) + EPS)
    return (y * g.astype(F32)).astype(x.dtype)


def _cmul(ar, ai, br, bi):
    return ar * br - ai * bi, ar * bi + ai * br


def _s5(u, a_re, a_im, log_dt, b_re, b_im, c_re, c_im, d_skip, h0_re, h0_im):
    n, L, _ = u.shape
    uf = u.astype(F32)
    ug = uf.reshape(n, L, N_SSM_GROUPS, SSM_STATE // SSM_STATE * SSM_GROUP)
    dt = jnp.exp(log_dt.astype(F32))[:, None]
    ar = a_re.astype(F32)
    ai = a_im.astype(F32)
    mag = jnp.exp(ar * dt)
    lb_re = mag * jnp.cos(ai * dt)
    lb_im = mag * jnp.sin(ai * dt)
    den = ar * ar + ai * ai
    nr = lb_re - 1.0
    f_re = (nr * ar + lb_im * ai) / den
    f_im = (lb_im * ar - nr * ai) / den
    bb_re, bb_im = _cmul(f_re[..., None], f_im[..., None], b_re.astype(F32), b_im.astype(F32))
    bu_re = jnp.einsum('nlgc,gpc->nlgp', ug, bb_re)
    bu_im = jnp.einsum('nlgc,gpc->nlgp', ug, bb_im)
    i_re, i_im = _cmul(lb_re, lb_im, h0_re.astype(F32), h0_im.astype(F32))
    bu_re = bu_re.at[:, 0].add(i_re)
    bu_im = bu_im.at[:, 0].add(i_im)
    a_full_re = jnp.broadcast_to(lb_re, bu_re.shape)
    a_full_im = jnp.broadcast_to(lb_im, bu_im.shape)

    def combine(e1, e2):
        a1r, a1i, b1r, b1i = e1
        a2r, a2i, b2r, b2i = e2
        nar, nai = _cmul(a2r, a2i, a1r, a1i)
        nbr, nbi = _cmul(a2r, a2i, b1r, b1i)
        return nar, nai, nbr + b2r, nbi + b2i

    _, _, h_re, h_im = lax.associative_scan(combine, (a_full_re, a_full_im, bu_re, bu_im), axis=1)
    y = (jnp.einsum('nlgp,gcp->nlgc', h_re, c_re.astype(F32))
         - jnp.einsum('nlgp,gcp->nlgc', h_im, c_im.astype(F32)))
    y = y.reshape(n, L, W_SSM) + d_skip.astype(F32) * uf
    return y.astype(u.dtype), h_re[:, -1].astype(h0_re.dtype), h_im[:, -1].astype(h0_im.dtype)


def _pool(u, buf, start_pos, w_pool, scale):
    n, L, _ = u.shape
    xx = jnp.concatenate([buf.astype(u.dtype), u], axis=1)
    cs = lax.cumsum(xx.astype(F32), axis=1)
    cs = jnp.concatenate([jnp.zeros((n, 1, W_POOL), F32), cs], axis=1)
    hi = cs[:, POOL_BUF + 1:POOL_BUF + 1 + L]
    t_abs = start_pos + jnp.arange(L)
    means = []
    for gi, w in enumerate(POOL_WINDOWS):
        sl = slice(gi * POOL_GROUP, (gi + 1) * POOL_GROUP)
        lo = cs[:, POOL_BUF + 1 - w:POOL_BUF + 1 - w + L, sl]
        cnt = jnp.minimum(t_abs + 1, w).astype(F32)[None, :, None]
        means.append((hi[..., sl] - lo) / cnt)
    d = jnp.concatenate(means, axis=-1) - u.astype(F32)
    y = jnp.einsum('nlgc,gce->nlge', d.reshape(n, L, N_POOL_GROUPS, POOL_GROUP), w_pool.astype(F32))
    y = y.reshape(n, L, W_POOL) * scale.astype(F32)
    return y.astype(u.dtype), xx[:, -POOL_BUF:].astype(buf.dtype)


def _diff_attention(q, k_sets, v_sets, kpos_sets, qpos, lam):
    n, lq = q.shape[:2]
    blk = min(Q_BLOCK, lq)
    pad = (-lq) % blk
    q = jnp.pad(q, ((0, 0), (0, pad), (0, 0), (0, 0), (0, 0)))
    qpos = jnp.pad(qpos, (0, pad), mode='edge')
    nb = (lq + pad) // blk
    qb = jnp.swapaxes(q.reshape(n, nb, blk, N_HEADS_ATT, 2, HEAD_DIM_ATT), 0, 1)
    pb = qpos.reshape(nb, blk)
    sizes = [k.shape[1] for k in k_sets]
    scale = HEAD_DIM_ATT ** -0.5

    def one(args):
        qi, pi = args
        scores = []
        for k, kp in zip(k_sets, kpos_sets):
            s = jnp.einsum('bqhcd,bkhcd->bhcqk', qi, k).astype(F32) * scale
            scores.append(jnp.where(pi[:, None] >= kp[None, :], s, NEG_INF))
        pr = jax.nn.softmax(jnp.concatenate(scores, axis=-1), axis=-1)
        w = pr[:, :, 0] - lam * pr[:, :, 1]
        out = None
        off = 0
        for v, sz in zip(v_sets, sizes):
            o = jnp.einsum('bhqk,bkhe->bqhe', w[..., off:off + sz].astype(v.dtype), v)
            out = o if out is None else out + o
            off += sz
        return out

    o = lax.map(one, (qb, pb))
    return jnp.swapaxes(o, 0, 1).reshape(n, nb * blk, N_HEADS_ATT, 2 * HEAD_DIM_ATT)[:, :lq]


def _mem_kv(mem, g, w_k, w_v):
    n = mem.shape[0]
    mn = _rmsnorm(mem, g)
    mk = (mn @ w_k).reshape(n, N_MEM, N_HEADS_MEM, HEAD_DIM_MEM)
    mv = (mn @ w_v).reshape(n, N_MEM, N_HEADS_MEM, HEAD_DIM_MEM)
    return mk, mv


def _cross_attn(xn, mk, mv, w_q, w_o):
    n, L, _ = xn.shape
    q = (xn @ w_q).reshape(n, L, N_HEADS_MEM, HEAD_DIM_MEM)
    s = jnp.einsum('nlhd,nmhd->nhlm', q, mk.astype(q.dtype)).astype(F32) * HEAD_DIM_MEM ** -0.5
    pr = jax.nn.softmax(s, axis=-1).astype(q.dtype)
    o = jnp.einsum('nhlm,nmhd->nlhd', pr, mv.astype(q.dtype)).reshape(n, L, D_MODEL)
    return o @ w_o


def _peer(xn, w_query, keys1, keys2, u_tab, v_tab):
    n, L, _ = xn.shape
    t = n * L
    xt = xn.reshape(t, D_MODEL)
    q = (xt @ w_query).astype(F32).reshape(t, N_RETR_HEADS, 2, D_QUERY // 2)
    s1 = jnp.einsum('thd,hkd->thk', q[:, :, 0], keys1.astype(F32))
    s2 = jnp.einsum('thd,hkd->thk', q[:, :, 1], keys2.astype(F32))
    v1, i1 = lax.top_k(s1, TOPK_HALF)
    v2, i2 = lax.top_k(s2, TOPK_HALF)
    cand_s = (v1[..., :, None] + v2[..., None, :]).reshape(t, N_RETR_HEADS, TOPK_HALF * TOPK_HALF)
    cand_i = (i1[..., :, None] * N_KEYS + i2[..., None, :]).reshape(t, N_RETR_HEADS, TOPK_HALF * TOPK_HALF)
    top_s, top_pos = lax.top_k(cand_s, TOPK)
    eidx = jnp.take_along_axis(cand_i, top_pos, axis=-1)
    gate = jax.nn.softmax(top_s, axis=-1)
    pad = (-t) % TOKEN_BLOCK
    nb = (t + pad) // TOKEN_BLOCK
    xb = jnp.pad(xt, ((0, pad), (0, 0))).reshape(nb, TOKEN_BLOCK, D_MODEL)
    eb = jnp.pad(eidx, ((0, pad), (0, 0), (0, 0))).reshape(nb, TOKEN_BLOCK, N_RETR_HEADS, TOPK)
    gb = jnp.pad(gate, ((0, pad), (0, 0), (0, 0))).reshape(nb, TOKEN_BLOCK, N_RETR_HEADS, TOPK)

    def block(args):
        xi, ei, gi = args
        h = jnp.einsum('td,thkd->thk', xi, u_tab[ei]).astype(F32)
        w = (gi * jax.nn.gelu(h, approximate=False)).astype(xi.dtype)
        return jnp.einsum('thk,thkd->td', w, v_tab[ei])

    y = lax.map(block, (xb, eb, gb))
    return y.reshape(nb * TOKEN_BLOCK, D_MODEL)[:t].reshape(n, L, D_MODEL)


def _layer(x, l, p, mem_k, mem_v, h0_re, h0_im, pool_buf, start_pos, past_k, past_v):
    n, L, _ = x.shape
    xn = _rmsnorm(x, p['norm_mix'][l])
    proj = xn @ p['w_in'][l]
    o1 = W_SSM
    o2 = o1 + W_POOL
    o3 = o2 + W_ATT
    o4 = o3 + W_ATT
    o5 = o4 + W_ATT
    u_ssm = proj[..., :o1]
    u_pool = proj[..., o1:o2]
    q = proj[..., o2:o3].reshape(n, L, N_HEADS_ATT, 2, HEAD_DIM_ATT)
    k = proj[..., o3:o4].reshape(n, L, N_HEADS_ATT, 2, HEAD_DIM_ATT)
    v = proj[..., o4:o5].reshape(n, L, N_HEADS_ATT, 2 * HEAD_DIM_ATT)
    gates = jax.nn.sigmoid(proj[..., o5:].astype(F32)).astype(x.dtype).reshape(n, L, 3, D_MODEL)
    y_a, h_re, h_im = _s5(u_ssm, p['ssm_a_re'][l], p['ssm_a_im'][l], p['ssm_log_dt'][l],
                          p['ssm_b_re'][l], p['ssm_b_im'][l], p['ssm_c_re'][l], p['ssm_c_im'][l],
                          p['ssm_d'][l], h0_re, h0_im)
    z = jax.nn.gelu(y_a, approximate=False)
    z = z * jax.nn.sigmoid(z @ p['ssm_w_glu'][l])
    y_b, new_buf = _pool(u_pool, pool_buf, start_pos, p['pool_w'][l], p['pool_scale'][l])
    lam_init = 0.8 - 0.6 * math.exp(-0.3 * l)
    lam = (jnp.exp(jnp.sum(p['att_lq1'][l].astype(F32) * p['att_lk1'][l].astype(F32)))
           - jnp.exp(jnp.sum(p['att_lq2'][l].astype(F32) * p['att_lk2'][l].astype(F32))) + lam_init)
    qpos = start_pos + jnp.arange(L)
    if past_k is None:
        k_sets, v_sets, kpos = [k], [v], [qpos]
    else:
        k_sets, v_sets, kpos = [past_k, k], [past_v, v], [jnp.arange(past_k.shape[1]), qpos]
    o = _diff_attention(q, k_sets, v_sets, kpos, qpos, lam)
    o = _rmsnorm(o, p['att_subln'][l]) * (1.0 - lam_init)
    y_c = o.reshape(n, L, W_ATT)
    merged = (gates[:, :, 0] * (z @ p['w_up_ssm'][l])
              + gates[:, :, 1] * (y_b @ p['w_up_pool'][l])
              + gates[:, :, 2] * (y_c @ p['w_up_att'][l]))
    x = x + merged @ p['w_out'][l]
    x = x + _cross_attn(_rmsnorm(x, p['norm_cross'][l]), mem_k, mem_v, p['w_mem_q'][l], p['w_mem_o'][l])
    x = x + _peer(_rmsnorm(x, p['norm_ffn'][l]), p['peer_w_query'][l], p['peer_keys1'][l],
                  p['peer_keys2'][l], p['peer_u'][l], p['peer_v'][l])
    new_k = k.reshape(n, L, N_HEADS_ATT, 2 * HEAD_DIM_ATT)
    return x, new_k, v, h_re, h_im, new_buf


def setup_inputs(seed: int = 0) -> dict:
    key = jax.random.key(seed)
    ks = iter(jax.random.split(key, 64))

    def nrm(shape, scale):
        return jax.random.normal(next(ks), shape, F32) * scale

    def gain(shape):
        return 1.0 + 0.02 * jax.random.normal(next(ks), shape, F32)

    n_pages = PAST_LEN // PAGE_SIZE
    n_used = DEC_BATCH * n_pages
    n_pool = n_used + max(1, n_used // 4)
    G, P = N_SSM_GROUPS, SSM_STATE
    D = D_MODEL
    return {
        'x_prompt': nrm((BATCH, SEQ, D), 1.0),
        'x_sample': nrm((DEC_BATCH, DEC_SEQ, D), 1.0),
        'mem_prompt': nrm((BATCH, N_MEM, D), 1.0),
        'cache_k': nrm((DEPTH, n_pool, PAGE_SIZE, N_HEADS_ATT, 2 * HEAD_DIM_ATT), 1.0),
        'cache_v': nrm((DEPTH, n_pool, PAGE_SIZE, N_HEADS_ATT, 2 * HEAD_DIM_ATT), 1.0),
        'cache_mem_k': nrm((DEPTH, DEC_BATCH, N_MEM, N_HEADS_MEM, HEAD_DIM_MEM), 1.0),
        'cache_mem_v': nrm((DEPTH, DEC_BATCH, N_MEM, N_HEADS_MEM, HEAD_DIM_MEM), 1.0),
        'state_ssm_re': nrm((DEPTH, DEC_BATCH, G, P), 0.5),
        'state_ssm_im': nrm((DEPTH, DEC_BATCH, G, P), 0.5),
        'state_pool': nrm((DEPTH, DEC_BATCH, POOL_BUF, W_POOL), 1.0),
        'page_table': jax.random.permutation(next(ks), n_pool)[:n_used].reshape(DEC_BATCH, n_pages).astype(jnp.int32),
        'norm_mix': gain((DEPTH, D)),
        'w_in': nrm((DEPTH, D, IN_COLS), D ** -0.5),
        'ssm_a_re': -0.5 + nrm((DEPTH, G, P), 0.01),
        'ssm_a_im': math.pi * jnp.arange(P, dtype=F32)[None, None, :] + nrm((DEPTH, G, P), 0.01),
        'ssm_log_dt': jax.random.uniform(next(ks), (DEPTH, G), F32, math.log(1e-3), math.log(1e-1)),
        'ssm_b_re': nrm((DEPTH, G, P, SSM_GROUP), (2 * SSM_GROUP) ** -0.5),
        'ssm_b_im': nrm((DEPTH, G, P, SSM_GROUP), (2 * SSM_GROUP) ** -0.5),
        'ssm_c_re': nrm((DEPTH, G, SSM_GROUP, P), (2 * P) ** -0.5),
        'ssm_c_im': nrm((DEPTH, G, SSM_GROUP, P), (2 * P) ** -0.5),
        'ssm_d': nrm((DEPTH, W_SSM), 1.0),
        'ssm_w_glu': nrm((DEPTH, W_SSM, W_SSM), W_SSM ** -0.5),
        'pool_w': nrm((DEPTH, N_POOL_GROUPS, POOL_GROUP, POOL_GROUP), POOL_GROUP ** -0.5),
        'pool_scale': 1.0 + nrm((DEPTH, W_POOL), 0.1),
        'att_lq1': nrm((DEPTH, HEAD_DIM_ATT), 0.1),
        'att_lk1': nrm((DEPTH, HEAD_DIM_ATT), 0.1),
        'att_lq2': nrm((DEPTH, HEAD_DIM_ATT), 0.1),
        'att_lk2': nrm((DEPTH, HEAD_DIM_ATT), 0.1),
        'att_subln': gain((DEPTH, 2 * HEAD_DIM_ATT)),
        'w_up_ssm': nrm((DEPTH, W_SSM, D), W_SSM ** -0.5),
        'w_up_pool': nrm((DEPTH, W_POOL, D), W_POOL ** -0.5),
        'w_up_att': nrm((DEPTH, W_ATT, D), W_ATT ** -0.5),
        'w_out': nrm((DEPTH, D, D), D ** -0.5),
        'norm_cross': gain((DEPTH, D)),
        'norm_mem': gain((DEPTH, D)),
        'w_mem_q': nrm((DEPTH, D, D), D ** -0.5),
        'w_mem_k': nrm((DEPTH, D, D), D ** -0.5),
        'w_mem_v': nrm((DEPTH, D, D), D ** -0.5),
        'w_mem_o': nrm((DEPTH, D, D), D ** -0.5),
        'norm_ffn': gain((DEPTH, D)),
        'peer_w_query': nrm((DEPTH, D, N_RETR_HEADS * D_QUERY), D ** -0.5),
        'peer_keys1': nrm((DEPTH, N_RETR_HEADS, N_KEYS, D_QUERY // 2), (D_QUERY // 2) ** -0.5),
        'peer_keys2': nrm((DEPTH, N_RETR_HEADS, N_KEYS, D_QUERY // 2), (D_QUERY // 2) ** -0.5),
        'peer_u': nrm((DEPTH, N_EXPERTS, D), D ** -0.5),
        'peer_v': nrm((DEPTH, N_EXPERTS, D), N_RETR_HEADS ** -0.5),
        'norm_final': gain((D,)),
    }


def reference(x_prompt, x_sample, mem_prompt, cache_k, cache_v, cache_mem_k, cache_mem_v,
              state_ssm_re, state_ssm_im, state_pool, page_table,
              norm_mix, w_in, ssm_a_re, ssm_a_im, ssm_log_dt, ssm_b_re, ssm_b_im, ssm_c_re, ssm_c_im,
              ssm_d, ssm_w_glu, pool_w, pool_scale, att_lq1, att_lk1, att_lq2, att_lk2, att_subln,
              w_up_ssm, w_up_pool, w_up_att, w_out, norm_cross, norm_mem, w_mem_q, w_mem_k, w_mem_v,
              w_mem_o, norm_ffn, peer_w_query, peer_keys1, peer_keys2, peer_u, peer_v, norm_final):
    p = dict(norm_mix=norm_mix, w_in=w_in, ssm_a_re=ssm_a_re, ssm_a_im=ssm_a_im, ssm_log_dt=ssm_log_dt,
             ssm_b_re=ssm_b_re, ssm_b_im=ssm_b_im, ssm_c_re=ssm_c_re, ssm_c_im=ssm_c_im, ssm_d=ssm_d,
             ssm_w_glu=ssm_w_glu, pool_w=pool_w, pool_scale=pool_scale, att_lq1=att_lq1, att_lk1=att_lk1,
             att_lq2=att_lq2, att_lk2=att_lk2, att_subln=att_subln, w_up_ssm=w_up_ssm,
             w_up_pool=w_up_pool, w_up_att=w_up_att, w_out=w_out, norm_cross=norm_cross,
             w_mem_q=w_mem_q, w_mem_o=w_mem_o, norm_ffn=norm_ffn, peer_w_query=peer_w_query,
             peer_keys1=peer_keys1, peer_keys2=peer_keys2, peer_u=peer_u, peer_v=peer_v)
    n_p = x_prompt.shape[0]
    n_s = x_sample.shape[0]
    past_len = page_table.shape[1] * cache_k.shape[2]
    yp = x_prompt
    ys = x_sample
    kp_l, vp_l, mkp_l, mvp_l, hrp_l, hip_l, bp_l = [], [], [], [], [], [], []
    ks_l, vs_l, hrs_l, his_l, bs_l = [], [], [], [], []
    h_zero = jnp.zeros((n_p, N_SSM_GROUPS, SSM_STATE), F32)
    buf_zero = jnp.zeros((n_p, POOL_BUF, W_POOL), x_prompt.dtype)
    for l in range(DEPTH):
        mk_p, mv_p = _mem_kv(mem_prompt, norm_mem[l], w_mem_k[l], w_mem_v[l])
        yp, kp, vp, hrp, hip, bp = _layer(yp, l, p, mk_p, mv_p, h_zero, h_zero, buf_zero, 0, None, None)
        kp_l.append(kp); vp_l.append(vp); mkp_l.append(mk_p); mvp_l.append(mv_p)
        hrp_l.append(hrp); hip_l.append(hip); bp_l.append(bp)
        k_past = cache_k[l, page_table].reshape(n_s, past_len, N_HEADS_ATT, 2, HEAD_DIM_ATT)
        v_past = cache_v[l, page_table].reshape(n_s, past_len, N_HEADS_ATT, 2 * HEAD_DIM_ATT)
        ys, ks_, vs_, hrs, his, bs = _layer(ys, l, p, cache_mem_k[l], cache_mem_v[l], state_ssm_re[l],
                                            state_ssm_im[l], state_pool[l], past_len, k_past, v_past)
        ks_l.append(ks_); vs_l.append(vs_); hrs_l.append(hrs); his_l.append(his); bs_l.append(bs)
    y_prompt = _rmsnorm(yp, norm_final)
    y_sample = _rmsnorm(ys, norm_final)
    return (y_prompt, y_sample,
            jnp.stack(kp_l), jnp.stack(vp_l), jnp.stack(mkp_l), jnp.stack(mvp_l),
            jnp.stack(hrp_l), jnp.stack(hip_l), jnp.stack(bp_l),
            jnp.stack(ks_l), jnp.stack(vs_l), jnp.stack(hrs_l), jnp.stack(his_l), jnp.stack(bs_l))
```

```python
import functools
import math

import jax
import jax.numpy as jnp
from jax import lax
from jax.experimental import pallas as pl
from jax.experimental.pallas import tpu as pltpu

F32 = jnp.float32
BF16 = jnp.bfloat16
EPS = 1e-6
NEG_INF = -1e30

D_MODEL = 2048
W_SSM = 512
SSM_GROUP = 16
N_SSM_GROUPS = 32
SSM_STATE = 64
N_STATE = N_SSM_GROUPS * SSM_STATE
W_POOL = 512
POOL_WINDOWS = (2, 4, 8, 16)
POOL_GROUP = 128
POOL_BUF = 15
POOL_CARRY = 16
N_HEADS_ATT = 8
HEAD_DIM_ATT = 64
W_HEAD = 2 * HEAD_DIM_ATT
W_ATT = N_HEADS_ATT * W_HEAD
N_MEM = 256
N_HEADS_MEM = 4
HEAD_DIM_MEM = 512
N_KEYS = 128
N_RETR_HEADS = 8
D_QUERY = 256
TOPK = 16

COL_SSM = 0
COL_POOL = W_SSM
COL_Q = W_SSM + W_POOL
COL_K = COL_Q + W_ATT
COL_V = COL_K + W_ATT
COL_GATE = COL_V + W_ATT
IN_COLS = COL_GATE + 3 * D_MODEL

VMEM_LIMIT_BYTES = 56 * 1024 * 1024
NT_DIMS = (((1,), (1,)), ((), ()))


def _cparams(*sem):
    return pltpu.CompilerParams(dimension_semantics=sem, vmem_limit_bytes=VMEM_LIMIT_BYTES)


def _bdot(a, b):
    return jnp.dot(a.astype(BF16), b.astype(BF16), preferred_element_type=F32)


def _bdot_nt(a, b):
    return lax.dot_general(a.astype(BF16), b.astype(BF16), NT_DIMS, preferred_element_type=F32)


def _rms(x, g):
    return x * lax.rsqrt(jnp.mean(x * x, axis=-1, keepdims=True) + EPS) * g


def _gelu(x):
    return 0.5 * x * (1.0 + lax.erf(x * (2.0 ** -0.5)))


def _in_proj_kernel(x_ref, g_ref, w_ref, a_ref, gate_ref, xn_ref, *, n_a_blocks):
    j = pl.program_id(1)

    @pl.when(j == 0)
    def _():
        xn_ref[...] = _rms(x_ref[...], g_ref[...]).astype(BF16)

    res = jnp.dot(xn_ref[...], w_ref[...].astype(BF16), preferred_element_type=F32)

    @pl.when(j < n_a_blocks)
    def _():
        a_ref[...] = res

    @pl.when(j >= n_a_blocks)
    def _():
        gate_ref[...] = jax.nn.sigmoid(res)


def _in_proj(x, g, w_in, l, *, tm, tn):
    t = x.shape[0]
    n_a = COL_GATE // tn
    n_j = IN_COLS // tn
    return pl.pallas_call(
        functools.partial(_in_proj_kernel, n_a_blocks=n_a),
        grid=(t // tm, n_j),
        in_specs=[
            pl.BlockSpec((tm, D_MODEL), lambda i, j: (i, 0)),
            pl.BlockSpec((1, D_MODEL), lambda i, j: (0, 0)),
            pl.BlockSpec((None, D_MODEL, tn), lambda i, j: (l, 0, j)),
        ],
        out_specs=[
            pl.BlockSpec((tm, tn), lambda i, j: (i, jnp.minimum(j, n_a - 1))),
            pl.BlockSpec((tm, tn), lambda i, j: (i, jnp.maximum(j - n_a, 0))),
        ],
        out_shape=[jax.ShapeDtypeStruct((t, COL_GATE), F32),
                   jax.ShapeDtypeStruct((t, 3 * D_MODEL), F32)],
        scratch_shapes=[pltpu.VMEM((tm, D_MODEL), BF16)],
        compiler_params=_cparams("parallel", "arbitrary"),
        name="in_proj",
    )(x, g.reshape(1, D_MODEL), w_in)


def _mm_kernel(*refs, has_norm, has_res):
    x_ref = refs[0]
    pos = 1
    g_ref = None
    if has_norm:
        g_ref = refs[pos]
        pos += 1
    w_ref = refs[pos]
    pos += 1
    r_ref = None
    if has_res:
        r_ref = refs[pos]
        pos += 1
    o_ref, xb_ref = refs[pos], refs[pos + 1]
    j = pl.program_id(1)

    @pl.when(j == 0)
    def _():
        x = x_ref[...]
        if has_norm:
            x = _rms(x, g_ref[...])
        xb_ref[...] = x.astype(BF16)

    res = jnp.dot(xb_ref[...], w_ref[...].astype(BF16), preferred_element_type=F32)
    if has_res:
        res = res + r_ref[...]
    o_ref[...] = res


def _matmul(x, w, l, *, gain=None, res=None, tm, tn, name):
    t, k = x.shape
    n = w.shape[2]
    has_norm = gain is not None
    has_res = res is not None
    in_specs = [pl.BlockSpec((tm, k), lambda i, j: (i, 0))]
    args = [x]
    if has_norm:
        in_specs.append(pl.BlockSpec((1, k), lambda i, j: (0, 0)))
        args.append(gain.reshape(1, k))
    in_specs.append(pl.BlockSpec((None, k, tn), lambda i, j: (l, 0, j)))
    args.append(w)
    if has_res:
        in_specs.append(pl.BlockSpec((tm, tn), lambda i, j: (i, j)))
        args.append(res)
    return pl.pallas_call(
        functools.partial(_mm_kernel, has_norm=has_norm, has_res=has_res),
        grid=(t // tm, n // tn),
        in_specs=in_specs,
        out_specs=pl.BlockSpec((tm, tn), lambda i, j: (i, j)),
        out_shape=jax.ShapeDtypeStruct((t, n), F32),
        scratch_shapes=[pltpu.VMEM((tm, k), BF16)],
        compiler_params=_cparams("parallel", "arbitrary"),
        name=name,
    )(*args)


def _merge_kernel(z_ref, yb_ref, yc_ref, g0_ref, g1_ref, g2_ref, wa_ref, wb_ref, wc_ref, o_ref):
    o_ref[...] = (g0_ref[...] * _bdot(z_ref[...], wa_ref[...])
                  + g1_ref[...] * _bdot(yb_ref[...], wb_ref[...])
                  + g2_ref[...] * _bdot(yc_ref[...], wc_ref[...]))


def _merge(z, yb, yc, gates, w_up_ssm, w_up_pool, w_up_att, l, *, tm, tn):
    t = z.shape[0]
    nb = D_MODEL // tn
    return pl.pallas_call(
        _merge_kernel,
        grid=(t // tm, nb),
        in_specs=[
            pl.BlockSpec((tm, W_SSM), lambda i, j: (i, 0)),
            pl.BlockSpec((tm, W_POOL), lambda i, j: (i, 0)),
            pl.BlockSpec((tm, W_ATT), lambda i, j: (i, 0)),
            pl.BlockSpec((tm, tn), lambda i, j: (i, j)),
            pl.BlockSpec((tm, tn), lambda i, j: (i, nb + j)),
            pl.BlockSpec((tm, tn), lambda i, j: (i, 2 * nb + j)),
            pl.BlockSpec((None, W_SSM, tn), lambda i, j: (l, 0, j)),
            pl.BlockSpec((None, W_POOL, tn), lambda i, j: (l, 0, j)),
            pl.BlockSpec((None, W_ATT, tn), lambda i, j: (l, 0, j)),
        ],
        out_specs=pl.BlockSpec((tm, tn), lambda i, j: (i, j)),
        out_shape=jax.ShapeDtypeStruct((t, D_MODEL), F32),
        compiler_params=_cparams("parallel", "arbitrary"),
        name="merge_up",
    )(z, yb, yc, gates, gates, gates, w_up_ssm, w_up_pool, w_up_att)


def _final_norm_kernel(x_ref, g_ref, o_ref):
    o_ref[...] = _rms(x_ref[...], g_ref[...])


def _final_norm(x, g, *, tm):
    t = x.shape[0]
    return pl.pallas_call(
        _final_norm_kernel,
        grid=(t // tm,),
        in_specs=[pl.BlockSpec((tm, D_MODEL), lambda i: (i, 0)),
                  pl.BlockSpec((1, D_MODEL), lambda i: (0, 0))],
        out_specs=pl.BlockSpec((tm, D_MODEL), lambda i: (i, 0)),
        out_shape=jax.ShapeDtypeStruct((t, D_MODEL), F32),
        compiler_params=_cparams("parallel"),
        name="final_norm",
    )(x, g.reshape(1, D_MODEL))


def _s5_tables(a_re, a_im, log_dt, b_re, b_im, c_re, c_im, tc):
    g_, p_ = N_SSM_GROUPS, SSM_STATE
    dt = jnp.exp(log_dt)[:, None]
    mag = jnp.exp(a_re * dt)
    lb_re = mag * jnp.cos(a_im * dt)
    lb_im = mag * jnp.sin(a_im * dt)
    den = a_re * a_re + a_im * a_im
    nr = lb_re - 1.0
    f_re = (nr * a_re + lb_im * a_im) / den
    f_im = (lb_im * a_re - nr * a_im) / den
    bb_re = f_re[..., None] * b_re - f_im[..., None] * b_im
    bb_im = f_re[..., None] * b_im + f_im[..., None] * b_re
    eye = jnp.eye(g_, dtype=F32)
    bd_re = jnp.einsum('gpc,gh->gchp', bb_re, eye).reshape(W_SSM, N_STATE)
    bd_im = jnp.einsum('gpc,gh->gchp', bb_im, eye).reshape(W_SSM, N_STATE)
    bmat = jnp.concatenate([bd_re, bd_im], axis=1).astype(BF16)
    cd_re = jnp.einsum('gcp,gh->gphc', c_re, eye).reshape(N_STATE, W_SSM).astype(BF16)
    cd_im = jnp.einsum('gcp,gh->gphc', -c_im, eye).reshape(N_STATE, W_SSM).astype(BF16)
    lr = lb_re.reshape(1, N_STATE)
    li = lb_im.reshape(1, N_STATE)
    nsteps = max(1, int(math.log2(tc)))
    sq_re, sq_im = [lr], [li]
    for _ in range(nsteps):
        r, i = sq_re[-1], sq_im[-1]
        sq_re.append(r * r - i * i)
        sq_im.append(2.0 * r * i)
    pt_re, pt_im = lr, li
    for s in range(nsteps):
        r, i = sq_re[s], sq_im[s]
        pt_re, pt_im = (jnp.concatenate([pt_re, pt_re * r - pt_im * i], axis=0),
                        jnp.concatenate([pt_im, pt_re * i + pt_im * r], axis=0))
    pw_re = jnp.concatenate(sq_re[:nsteps], axis=0)
    pw_im = jnp.concatenate(sq_im[:nsteps], axis=0)
    return bmat, cd_re, cd_im, pw_re, pw_im, pt_re[:tc], pt_im[:tc]


def _s5_kernel(u_ref, b_ref, cre_ref, cim_ref, d_ref, pwr_ref, pwi_ref, ptr_ref, pti_ref,
               h0r_ref, h0i_ref, wglu_ref, z_ref, hr_ref, hi_ref, cr_ref, ci_ref, *, tc, nsteps):
    c = pl.program_id(1)

    @pl.when(c == 0)
    def _():
        cr_ref[...] = h0r_ref[...]
        ci_ref[...] = h0i_ref[...]

    u = u_ref[...]
    bu = jnp.dot(u.astype(BF16), b_ref[...], preferred_element_type=F32)
    re = bu[:, :N_STATE]
    im = bu[:, N_STATE:]
    row = lax.broadcasted_iota(jnp.int32, (tc, 1), 0)
    for s in range(nsteps):
        k = 1 << s
        keep = row >= k
        sre = jnp.where(keep, pltpu.roll(re, k, 0), 0.0)
        sim = jnp.where(keep, pltpu.roll(im, k, 0), 0.0)
        ar = pwr_ref[s:s + 1, :]
        ai = pwi_ref[s:s + 1, :]
        re, im = re + (ar * sre - ai * sim), im + (ar * sim + ai * sre)
    hr = cr_ref[...]
    hi = ci_ref[...]
    pr = ptr_ref[...]
    pi = pti_ref[...]
    re = re + (pr * hr - pi * hi)
    im = im + (pr * hi + pi * hr)
    cr_ref[...] = re[tc - 1:tc, :]
    ci_ref[...] = im[tc - 1:tc, :]
    y = (jnp.dot(re.astype(BF16), cre_ref[...], preferred_element_type=F32)
         + jnp.dot(im.astype(BF16), cim_ref[...], preferred_element_type=F32)
         + d_ref[...] * u)
    z = _gelu(y)
    z_ref[...] = z * jax.nn.sigmoid(_bdot(z, wglu_ref[...]))

    @pl.when(c == pl.num_programs(1) - 1)
    def _():
        hr_ref[...] = re[tc - 1:tc, :]
        hi_ref[...] = im[tc - 1:tc, :]


def _s5(a_mat, row0, n_seq, seq_len, tables, d_skip, h0_re, h0_im, w_glu, l, *, tc):
    bmat, cd_re, cd_im, pw_re, pw_im, pt_re, pt_im = tables
    nsteps = pw_re.shape[0] if tc > 1 else 0
    nc = seq_len // tc
    rb0 = row0 // tc
    const = lambda shape: pl.BlockSpec(shape, lambda n, c: tuple(0 for _ in shape))
    z, hr, hi = pl.pallas_call(
        functools.partial(_s5_kernel, tc=tc, nsteps=nsteps),
        grid=(n_seq, nc),
        in_specs=[
            pl.BlockSpec((tc, W_SSM), lambda n, c: (rb0 + n * nc + c, COL_SSM // W_SSM)),
            const((W_SSM, 2 * N_STATE)),
            const((N_STATE, W_SSM)),
            const((N_STATE, W_SSM)),
            const((1, W_SSM)),
            const(pw_re.shape),
            const(pw_im.shape),
            const((tc, N_STATE)),
            const((tc, N_STATE)),
            pl.BlockSpec((None, 1, N_STATE), lambda n, c: (n, 0, 0)),
            pl.BlockSpec((None, 1, N_STATE), lambda n, c: (n, 0, 0)),
            pl.BlockSpec((None, W_SSM, W_SSM), lambda n, c: (l, 0, 0)),
        ],
        out_specs=[
            pl.BlockSpec((tc, W_SSM), lambda n, c: (n * nc + c, 0)),
            pl.BlockSpec((None, 1, N_STATE), lambda n, c: (n, 0, 0)),
            pl.BlockSpec((None, 1, N_STATE), lambda n, c: (n, 0, 0)),
        ],
        out_shape=[jax.ShapeDtypeStruct((n_seq * seq_len, W_SSM), F32),
                   jax.ShapeDtypeStruct((n_seq, 1, N_STATE), F32),
                   jax.ShapeDtypeStruct((n_seq, 1, N_STATE), F32)],
        scratch_shapes=[pltpu.VMEM((1, N_STATE), F32), pltpu.VMEM((1, N_STATE), F32)],
        compiler_params=_cparams("parallel", "arbitrary"),
        name="s5",
    )(a_mat, bmat, cd_re, cd_im, d_skip.reshape(1, W_SSM), pw_re, pw_im, pt_re, pt_im,
      h0_re.reshape(n_seq, 1, N_STATE), h0_im.reshape(n_seq, 1, N_STATE), w_glu)
    shp = (n_seq, N_SSM_GROUPS, SSM_STATE)
    return z, hr.reshape(shp), hi.reshape(shp)


def _pool_kernel(u_ref, buf_ref, w_ref, sc_ref, y_ref, nb_ref, xx_ref, *, tc, start_pos):
    c = pl.program_id(1)

    @pl.when(c == 0)
    def _():
        xx_ref[0:POOL_CARRY, :] = buf_ref[...]

    u = u_ref[...]
    xx_ref[POOL_CARRY:, :] = u
    x = xx_ref[...]
    sums = []
    s = x
    for w in (1, 2, 4, 8):
        s = s + pltpu.roll(s, w, 0)
        sums.append(s)
    t_abs = (start_pos + c * tc + lax.broadcasted_iota(jnp.int32, (tc, 1), 0) + 1).astype(F32)
    outs = []
    for gi, w in enumerate(POOL_WINDOWS):
        sl = slice(gi * POOL_GROUP, (gi + 1) * POOL_GROUP)
        mean = sums[gi][POOL_CARRY:, sl] / jnp.minimum(t_abs, float(w))
        d = mean - u[:, sl]
        outs.append(_bdot(d, w_ref[gi]))
    y_ref[...] = jnp.concatenate(outs, axis=-1) * sc_ref[...]
    carry = x[tc:, :]
    xx_ref[0:POOL_CARRY, :] = carry

    @pl.when(c == pl.num_programs(1) - 1)
    def _():
        nb_ref[...] = carry


def _pool(a_mat, row0, n_seq, seq_len, buf16, pool_w, pool_scale, l, start_pos, *, tc):
    nc = seq_len // tc
    rb0 = row0 // tc
    y, nb = pl.pallas_call(
        functools.partial(_pool_kernel, tc=tc, start_pos=start_pos),
        grid=(n_seq, nc),
        in_specs=[
            pl.BlockSpec((tc, W_POOL), lambda n, c: (rb0 + n * nc + c, COL_POOL // W_POOL)),
            pl.BlockSpec((None, POOL_CARRY, W_POOL), lambda n, c: (n, 0, 0)),
            pl.BlockSpec((None, len(POOL_WINDOWS), POOL_GROUP, POOL_GROUP), lambda n, c: (l, 0, 0, 0)),
            pl.BlockSpec((1, W_POOL), lambda n, c: (0, 0)),
        ],
        out_specs=[
            pl.BlockSpec((tc, W_POOL), lambda n, c: (n * nc + c, 0)),
            pl.BlockSpec((None, POOL_CARRY, W_POOL), lambda n, c: (n, 0, 0)),
        ],
        out_shape=[jax.ShapeDtypeStruct((n_seq * seq_len, W_POOL), F32),
                   jax.ShapeDtypeStruct((n_seq, POOL_CARRY, W_POOL), F32)],
        scratch_shapes=[pltpu.VMEM((POOL_CARRY + tc, W_POOL), F32)],
        compiler_params=_cparams("parallel", "arbitrary"),
        name="pool",
    )(a_mat, buf16, pool_w, pool_scale.reshape(1, W_POOL))
    return y, nb[:, POOL_CARRY - POOL_BUF:, :]


def _subln(o, g, lam_init):
    return _rms(o, g) * (1.0 - lam_init)


def _attn_prompt_kernel(qt_ref, kt_ref, q_ref, k_ref, v_ref, lam_ref, g_ref, o_ref,
                        m_ref, l_ref, acc_ref, *, tq, tk, lam_init):
    p = pl.program_id(2)
    qi = qt_ref[p]
    ki = kt_ref[p]

    @pl.when(ki == 0)
    def _():
        m_ref[...] = jnp.full(m_ref.shape, NEG_INF, F32)
        l_ref[...] = jnp.zeros(l_ref.shape, F32)
        acc_ref[...] = jnp.zeros(acc_ref.shape, F32)

    q = q_ref[...]
    k = k_ref[...]
    v = v_ref[...].astype(BF16)
    qpos = qi * tq + lax.broadcasted_iota(jnp.int32, (tq, tk), 0)
    kpos = ki * tk + lax.broadcasted_iota(jnp.int32, (tq, tk), 1)
    causal = qpos >= kpos
    scale = HEAD_DIM_ATT ** -0.5
    for c in range(2):
        sl = slice(c * HEAD_DIM_ATT, (c + 1) * HEAD_DIM_ATT)
        s = _bdot_nt(q[:, sl], k[:, sl]) * scale
        s = jnp.where(causal, s, NEG_INF)
        m_prev = m_ref[c]
        m_new = jnp.maximum(m_prev, jnp.max(s, axis=-1, keepdims=True))
        alpha = jnp.exp(m_prev - m_new)
        pexp = jnp.exp(s - m_new)
        l_ref[c] = alpha * l_ref[c] + jnp.sum(pexp, axis=-1, keepdims=True)
        acc_ref[c] = alpha * acc_ref[c] + jnp.dot(pexp.astype(BF16), v, preferred_element_type=F32)
        m_ref[c] = m_new

    @pl.when(ki == qi)
    def _():
        o = acc_ref[0] / l_ref[0] - lam_ref[0:1, :] * (acc_ref[1] / l_ref[1])
        o_ref[...] = _subln(o, g_ref[...], lam_init)


def _attn_prompt(a_mat, n_seq, seq_len, lam_row, subln_g, lam_init, *, tq):
    nq = seq_len // tq
    pairs = [(qi, ki) for qi in range(nq) for ki in range(qi + 1)]
    qt = jnp.asarray([p[0] for p in pairs], jnp.int32)
    kt = jnp.asarray([p[1] for p in pairs], jnp.int32)
    cq, ck, cv = COL_Q // W_HEAD, COL_K // W_HEAD, COL_V // W_HEAD
    grid_spec = pltpu.PrefetchScalarGridSpec(
        num_scalar_prefetch=2,
        grid=(n_seq, N_HEADS_ATT, len(pairs)),
        in_specs=[
            pl.BlockSpec((tq, W_HEAD), lambda b, h, p, qt, kt: (b * nq + qt[p], cq + h)),
            pl.BlockSpec((tq, W_HEAD), lambda b, h, p, qt, kt: (b * nq + kt[p], ck + h)),
            pl.BlockSpec((tq, W_HEAD), lambda b, h, p, qt, kt: (b * nq + kt[p], cv + h)),
            pl.BlockSpec((8, W_HEAD), lambda b, h, p, qt, kt: (0, 0)),
            pl.BlockSpec((1, W_HEAD), lambda b, h, p, qt, kt: (0, 0)),
        ],
        out_specs=pl.BlockSpec((tq, W_HEAD), lambda b, h, p, qt, kt: (b * nq + qt[p], h)),
        scratch_shapes=[pltpu.VMEM((2, tq, 1), F32), pltpu.VMEM((2, tq, 1), F32),
                        pltpu.VMEM((2, tq, W_HEAD), F32)],
    )
    return pl.pallas_call(
        functools.partial(_attn_prompt_kernel, tq=tq, tk=tq, lam_init=lam_init),
        grid_spec=grid_spec,
        out_shape=jax.ShapeDtypeStruct((n_seq * seq_len, W_ATT), F32),
        compiler_params=_cparams("parallel", "parallel", "arbitrary"),
        name="attn_prompt",
    )(qt, kt, a_mat, a_mat, a_mat, lam_row, subln_g.reshape(1, W_HEAD))


def _attn_sample_kernel(pt_ref, q_ref, kn_ref, vn_ref, lam_ref, g_ref, *rest, n_pages, dec_seq, lam_init):
    k_refs = rest[:n_pages]
    v_refs = rest[n_pages:2 * n_pages]
    o_ref, qbd_ref, m_ref, l_ref, acc_ref = rest[2 * n_pages:]
    step = pl.program_id(1)
    n_rows = 2 * N_HEADS_ATT * dec_seq
    scale = HEAD_DIM_ATT ** -0.5

    @pl.when(step == 0)
    def _():
        qrep = jnp.concatenate([q_ref[...]] * (2 * N_HEADS_ATT), axis=0)
        row = lax.broadcasted_iota(jnp.int32, (n_rows, W_ATT), 0)
        lane = lax.broadcasted_iota(jnp.int32, (n_rows, W_ATT), 1)
        qbd_ref[...] = jnp.where(lane // HEAD_DIM_ATT == row // dec_seq, qrep, 0.0).astype(BF16)
        m_ref[...] = jnp.full(m_ref.shape, NEG_INF, F32)
        l_ref[...] = jnp.zeros(l_ref.shape, F32)
        acc_ref[...] = jnp.zeros(acc_ref.shape, F32)

    def update(s, vcat):
        m_prev = m_ref[...]
        m_new = jnp.maximum(m_prev, jnp.max(s, axis=-1, keepdims=True))
        alpha = jnp.exp(m_prev - m_new)
        pexp = jnp.exp(s - m_new)
        l_ref[...] = alpha * l_ref[...] + jnp.sum(pexp, axis=-1, keepdims=True)
        pv = jnp.dot(pexp.astype(BF16), vcat, preferred_element_type=F32)
        rph = 2 * dec_seq
        diag = jnp.concatenate(
            [pv[h * rph:(h + 1) * rph, h * W_HEAD:(h + 1) * W_HEAD] for h in range(N_HEADS_ATT)], axis=0)
        acc_ref[...] = alpha * acc_ref[...] + diag
        m_ref[...] = m_new

    qbd = qbd_ref[...]
    kcat = jnp.concatenate([r[...].astype(BF16) for r in k_refs], axis=0)
    vcat = jnp.concatenate([r[...].astype(BF16) for r in v_refs], axis=0)
    s = lax.dot_general(qbd, kcat, NT_DIMS, preferred_element_type=F32) * scale
    update(s, vcat)

    @pl.when(step == pl.num_programs(1) - 1)
    def _():
        pad = jnp.zeros((W_HEAD - dec_seq, W_ATT), F32)
        kn = jnp.concatenate([kn_ref[...], pad], axis=0).astype(BF16)
        vn = jnp.concatenate([vn_ref[...], pad], axis=0).astype(BF16)
        s2 = lax.dot_general(qbd, kn, NT_DIMS, preferred_element_type=F32) * scale
        qi = lax.broadcasted_iota(jnp.int32, (n_rows, W_HEAD), 0) % dec_seq
        kj = lax.broadcasted_iota(jnp.int32, (n_rows, W_HEAD), 1)
        s2 = jnp.where(qi >= kj, s2, NEG_INF)
        update(s2, vn)
        o = acc_ref[...] / l_ref[...]
        lam = lam_ref[0:1, :]
        for h in range(N_HEADS_ATT):
            r0 = h * 2 * dec_seq
            oh = o[r0:r0 + dec_seq] - lam * o[r0 + dec_seq:r0 + 2 * dec_seq]
            o_ref[:, h * W_HEAD:(h + 1) * W_HEAD] = _subln(oh, g_ref[...], lam_init)


def _attn_sample(a_mat, row0, n_seq, dec_seq, cache_k, cache_v, page_table, lam_row, subln_g,
                 lam_init, l, *, pages_per_step):
    depth, n_pool, page, _, _ = cache_k.shape
    ck = cache_k.reshape(depth, n_pool, page, W_ATT)
    cv = cache_v.reshape(depth, n_pool, page, W_ATT)
    n_pt = page_table.shape[1]
    r = pages_per_step
    rb0 = row0 // dec_seq
    cq, ckk, cvv = COL_Q // W_ATT, COL_K // W_ATT, COL_V // W_ATT

    def page_spec(j):
        return pl.BlockSpec((None, None, page, W_ATT),
                            lambda n, s, pt: (l, pt[n, s * r + j], 0, 0))

    grid_spec = pltpu.PrefetchScalarGridSpec(
        num_scalar_prefetch=1,
        grid=(n_seq, n_pt // r),
        in_specs=[
            pl.BlockSpec((dec_seq, W_ATT), lambda n, s, pt: (rb0 + n, cq)),
            pl.BlockSpec((dec_seq, W_ATT), lambda n, s, pt: (rb0 + n, ckk)),
            pl.BlockSpec((dec_seq, W_ATT), lambda n, s, pt: (rb0 + n, cvv)),
            pl.BlockSpec((8, W_HEAD), lambda n, s, pt: (0, 0)),
            pl.BlockSpec((1, W_HEAD), lambda n, s, pt: (0, 0)),
        ] + [page_spec(j) for j in range(r)] + [page_spec(j) for j in range(r)],
        out_specs=pl.BlockSpec((dec_seq, W_ATT), lambda n, s, pt: (n, 0)),
        scratch_shapes=[pltpu.VMEM((2 * N_HEADS_ATT * dec_seq, W_ATT), BF16),
                        pltpu.VMEM((2 * N_HEADS_ATT * dec_seq, 1), F32),
                        pltpu.VMEM((2 * N_HEADS_ATT * dec_seq, 1), F32),
                        pltpu.VMEM((2 * N_HEADS_ATT * dec_seq, W_HEAD), F32)],
    )
    return pl.pallas_call(
        functools.partial(_attn_sample_kernel, n_pages=r, dec_seq=dec_seq, lam_init=lam_init),
        grid_spec=grid_spec,
        out_shape=jax.ShapeDtypeStruct((n_seq * dec_seq, W_ATT), F32),
        compiler_params=_cparams("parallel", "arbitrary"),
        name="attn_sample",
    )(page_table, a_mat, a_mat, a_mat, lam_row, subln_g.reshape(1, W_HEAD),
      *([ck] * r), *([cv] * r))


def _cross_kernel(q_ref, k_ref, v_ref, o_ref):
    q = q_ref[...]
    scale = HEAD_DIM_MEM ** -0.5
    for h in range(N_HEADS_MEM):
        sl = slice(h * HEAD_DIM_MEM, (h + 1) * HEAD_DIM_MEM)
        s = _bdot_nt(q[:, sl], k_ref[:, sl]) * scale
        s = s - jnp.max(s, axis=-1, keepdims=True)
        e = jnp.exp(s)
        pr = e / jnp.sum(e, axis=-1, keepdims=True)
        o_ref[:, sl] = _bdot(pr, v_ref[:, sl])


def _cross_attn(qm, row0, n_seq, seq_len, mem_k, mem_v, kv_index, *, tq):
    nq = seq_len // tq
    rb0 = row0 // tq
    blk = tuple(None for _ in range(mem_k.ndim - 2)) + (N_MEM, D_MODEL)
    return pl.pallas_call(
        _cross_kernel,
        grid=(n_seq, nq),
        in_specs=[
            pl.BlockSpec((tq, D_MODEL), lambda n, i: (rb0 + n * nq + i, 0)),
            pl.BlockSpec(blk, lambda n, i: kv_index(n)),
            pl.BlockSpec(blk, lambda n, i: kv_index(n)),
        ],
        out_specs=pl.BlockSpec((tq, D_MODEL), lambda n, i: (n * nq + i, 0)),
        out_shape=jax.ShapeDtypeStruct((n_seq * seq_len, D_MODEL), F32),
        compiler_params=_cparams("parallel", "arbitrary"),
        name="cross_attn",
    )(qm, mem_k, mem_v)


def _oddeven_sort_pairs(n):
    pairs = []

    def merge(lo, hi, r):
        step = r * 2
        if step < hi - lo:
            merge(lo, hi, step)
            merge(lo + r, hi, step)
            pairs.extend((i, i + r) for i in range(lo + r, hi - r, step))
        else:
            pairs.append((lo, lo + r))

    def sort(lo, hi):
        if hi - lo >= 1:
            mid = lo + (hi - lo) // 2
            sort(lo, mid)
            sort(mid + 1, hi)
            merge(lo, hi, 1)

    sort(0, n - 1)
    return pairs


_SORT16 = _oddeven_sort_pairs(TOPK)


def _cmpx(xs, i, j):
    a, b = xs[i], xs[j]
    if a is None:
        xs[i], xs[j] = b, None
    elif b is not None:
        xs[i], xs[j] = jnp.maximum(a, b), jnp.minimum(a, b)


def _sort16_desc(xs):
    xs = list(xs)
    for i, j in _SORT16:
        _cmpx(xs, i, j)
    return xs


def _merge_top16(a, b):
    c = []
    for i in range(TOPK):
        x, y = a[i], b[TOPK - 1 - i]
        c.append(y if x is None else (x if y is None else jnp.maximum(x, y)))
    stride = TOPK // 2
    while stride:
        for i in range(TOPK):
            if not i & stride:
                _cmpx(c, i, i + stride)
        stride //= 2
    return c


def _top16_rows(st):
    lists = _sort16_desc([st[8 * v:8 * v + 8, :] for v in range(N_KEYS // 8)])
    for shift in (4, 2, 1):
        lists = _merge_top16(lists, [pltpu.roll(x, shift, 0) for x in lists])
    return lists


_CAND_PAIRS = [(i, j) for i in range(TOPK) for j in range(TOPK) if (i + 1) * (j + 1) <= TOPK]


def _peer_score_kernel(x_ref, g_ref, wq_ref, k1_ref, k2_ref,
                       xn_ref, s1_ref, p1_ref, s2_ref, p2_ref, tau_ref):
    h = pl.program_id(1)

    @pl.when(h == 0)
    def _():
        xn_ref[...] = _rms(x_ref[...], g_ref[...]).astype(BF16)

    q = jnp.dot(xn_ref[...], wq_ref[...].astype(BF16), preferred_element_type=F32)
    half = D_QUERY // 2
    s1 = _bdot_nt(k1_ref[...], q[:, :half])
    s2 = _bdot_nt(k2_ref[...], q[:, half:])
    v1 = _top16_rows(s1)
    v2 = _top16_rows(s2)
    cands = [v1[i] + v2[j] for i, j in _CAND_PAIRS]
    cands += [None] * (-len(cands) % TOPK)
    groups = [_sort16_desc(cands[g:g + TOPK]) for g in range(0, len(cands), TOPK)]
    top = groups[0]
    for grp in groups[1:]:
        top = _merge_top16(top, grp)
    z = jnp.exp(top[0] - top[0])
    for t in top[1:]:
        z = z + jnp.exp(t - top[0])
    s1_ref[...] = s1
    s2_ref[...] = s2
    p1_ref[...] = jnp.exp(s1 - v1[0][0:1, :]) / z[0:1, :]
    p2_ref[...] = jnp.exp(s2 - v2[0][0:1, :])
    tau_ref[...] = top[TOPK - 1]


def _peer_score(x, g, w_query, keys1, keys2, l, *, tm):
    t = x.shape[0]
    nh = N_RETR_HEADS
    key_spec = pl.BlockSpec((None, None, N_KEYS, D_QUERY // 2), lambda i, h: (l, h, 0, 0))
    st_spec = pl.BlockSpec((None, N_KEYS, tm), lambda i, h: (h, 0, i))
    st_shape = jax.ShapeDtypeStruct((nh, N_KEYS, t), F32)
    return pl.pallas_call(
        _peer_score_kernel,
        grid=(t // tm, nh),
        in_specs=[
            pl.BlockSpec((tm, D_MODEL), lambda i, h: (i, 0)),
            pl.BlockSpec((1, D_MODEL), lambda i, h: (0, 0)),
            pl.BlockSpec((None, D_MODEL, D_QUERY), lambda i, h: (l, 0, h)),
            key_spec, key_spec,
        ],
        out_specs=[
            pl.BlockSpec((tm, D_MODEL), lambda i, h: (i, 0)),
            st_spec, st_spec, st_spec, st_spec,
            pl.BlockSpec((None, 8, tm), lambda i, h: (h, 0, i)),
        ],
        out_shape=[jax.ShapeDtypeStruct((t, D_MODEL), BF16),
                   st_shape, st_shape, st_shape, st_shape,
                   jax.ShapeDtypeStruct((nh, 8, t), F32)],
        compiler_params=_cparams("parallel", "arbitrary"),
        name="peer_score",
    )(x, g.reshape(1, D_MODEL), w_query, keys1, keys2)


def _peer_dense_kernel(xn_ref, u_ref, v_ref, s1_ref, p1_ref, s2_ref, p2_ref, tau_ref, x_ref,
                       o_ref, g_ref, *, n_a):
    e = pl.program_id(1)

    @pl.when(e == 0)
    def _():
        o_ref[...] = x_ref[...]

    ht = _bdot_nt(u_ref[...], xn_ref[...])

    def build(al, carry):
        g = jnp.zeros((N_KEYS, g_ref.shape[1]), F32)
        for h in range(N_RETR_HEADS):
            s = s1_ref[h, pl.ds(al, 1), :] + s2_ref[h]
            w = p1_ref[h, pl.ds(al, 1), :] * p2_ref[h]
            g = g + jnp.where(s >= tau_ref[h, 0:1, :], w, 0.0)
        g_ref[pl.ds(pl.multiple_of(al * N_KEYS, N_KEYS), N_KEYS), :] = g
        return carry

    lax.fori_loop(0, n_a, build, 0)
    wt = g_ref[...] * _gelu(ht)
    o_ref[...] += jnp.dot(wt.T.astype(BF16), v_ref[...].astype(BF16), preferred_element_type=F32)


def _peer_dense(xn, x_res, score, peer_u, peer_v, l, *, tm, te):
    s1t, p1t, s2t, p2t, tau = score
    t = xn.shape[0]
    n_exp = peer_u.shape[1]
    n_a = te // N_KEYS
    nh = N_RETR_HEADS
    return pl.pallas_call(
        functools.partial(_peer_dense_kernel, n_a=n_a),
        grid=(t // tm, n_exp // te),
        in_specs=[
            pl.BlockSpec((tm, D_MODEL), lambda i, e: (i, 0)),
            pl.BlockSpec((None, te, D_MODEL), lambda i, e: (l, e, 0)),
            pl.BlockSpec((None, te, D_MODEL), lambda i, e: (l, e, 0)),
            pl.BlockSpec((nh, n_a, tm), lambda i, e: (0, e, i)),
            pl.BlockSpec((nh, n_a, tm), lambda i, e: (0, e, i)),
            pl.BlockSpec((nh, N_KEYS, tm), lambda i, e: (0, 0, i)),
            pl.BlockSpec((nh, N_KEYS, tm), lambda i, e: (0, 0, i)),
            pl.BlockSpec((nh, 8, tm), lambda i, e: (0, 0, i)),
            pl.BlockSpec((tm, D_MODEL), lambda i, e: (i, 0)),
        ],
        out_specs=pl.BlockSpec((tm, D_MODEL), lambda i, e: (i, 0)),
        out_shape=jax.ShapeDtypeStruct((t, D_MODEL), F32),
        scratch_shapes=[pltpu.VMEM((te, tm), F32)],
        compiler_params=_cparams("parallel", "arbitrary"),
        name="peer_dense",
    )(xn, peer_u, peer_v, s1t, p1t, s2t, p2t, tau, x_res)


def kernel(x_prompt, x_sample, mem_prompt, cache_k, cache_v, cache_mem_k, cache_mem_v, state_ssm_re, state_ssm_im, state_pool, page_table, norm_mix, w_in, ssm_a_re, ssm_a_im, ssm_log_dt, ssm_b_re, ssm_b_im, ssm_c_re, ssm_c_im, ssm_d, ssm_w_glu, pool_w, pool_scale, att_lq1, att_lk1, att_lq2, att_lk2, att_subln, w_up_ssm, w_up_pool, w_up_att, w_out, norm_cross, norm_mem, w_mem_q, w_mem_k, w_mem_v, w_mem_o, norm_ffn, peer_w_query, peer_keys1, peer_keys2, peer_u, peer_v, norm_final):
    n_p, seq, d = x_prompt.shape
    n_s, dec_seq, _ = x_sample.shape
    depth = w_in.shape[0]
    tp = n_p * seq
    ts = n_s * dec_seq
    past_len = page_table.shape[1] * cache_k.shape[2]

    tm = 768
    tc_p, tc_s = 128, dec_seq

    x = jnp.concatenate([x_prompt.reshape(tp, d), x_sample.reshape(ts, d)], axis=0)
    mem_rows = mem_prompt.reshape(n_p * N_MEM, d)
    cmk = cache_mem_k.reshape(depth, n_s, N_MEM, d)
    cmv = cache_mem_v.reshape(depth, n_s, N_MEM, d)
    zeros_h = jnp.zeros((n_p, N_STATE), F32)
    zeros_buf = jnp.zeros((n_p, POOL_CARRY, W_POOL), F32)
    sample_buf = jnp.pad(state_pool, ((0, 0), (0, 0), (POOL_CARRY - POOL_BUF, 0), (0, 0)))
    peer_u = peer_u.astype(BF16)
    peer_v = peer_v.astype(BF16)

    outs = {k: [] for k in ("kp", "vp", "mkp", "mvp", "hrp", "hip", "bp", "ks", "vs", "hrs", "his", "bs")}
    for l in range(depth):
        a_mat, gates = _in_proj(x, norm_mix[l], w_in, l, tm=tm, tn=512)
        k_all = a_mat[:, COL_K:COL_V]
        v_all = a_mat[:, COL_V:COL_GATE]
        outs["kp"].append(k_all[:tp].reshape(n_p, seq, N_HEADS_ATT, W_HEAD))
        outs["vp"].append(v_all[:tp].reshape(n_p, seq, N_HEADS_ATT, W_HEAD))
        outs["ks"].append(k_all[tp:].reshape(n_s, dec_seq, N_HEADS_ATT, W_HEAD))
        outs["vs"].append(v_all[tp:].reshape(n_s, dec_seq, N_HEADS_ATT, W_HEAD))

        s5_params = (ssm_a_re[l], ssm_a_im[l], ssm_log_dt[l], ssm_b_re[l], ssm_b_im[l],
                     ssm_c_re[l], ssm_c_im[l])
        z_p, hrp, hip = _s5(a_mat, 0, n_p, seq, _s5_tables(*s5_params, tc_p), ssm_d[l],
                            zeros_h, zeros_h, ssm_w_glu, l, tc=tc_p)
        z_s, hrs, his = _s5(a_mat, tp, n_s, dec_seq, _s5_tables(*s5_params, tc_s), ssm_d[l],
                            state_ssm_re[l].reshape(n_s, N_STATE), state_ssm_im[l].reshape(n_s, N_STATE),
                            ssm_w_glu, l, tc=tc_s)
        outs["hrp"].append(hrp); outs["hip"].append(hip)
        outs["hrs"].append(hrs); outs["his"].append(his)

        yb_p, bp = _pool(a_mat, 0, n_p, seq, zeros_buf, pool_w, pool_scale[l], l, 0, tc=tc_p)
        yb_s, bs = _pool(a_mat, tp, n_s, dec_seq, sample_buf[l], pool_w, pool_scale[l], l, past_len, tc=tc_s)
        outs["bp"].append(bp); outs["bs"].append(bs)

        lam_init = 0.8 - 0.6 * math.exp(-0.3 * l)
        lam = (jnp.exp(jnp.sum(att_lq1[l] * att_lk1[l])) - jnp.exp(jnp.sum(att_lq2[l] * att_lk2[l])) + lam_init)
        lam_row = jnp.full((8, W_HEAD), lam, F32)
        yc_p = _attn_prompt(a_mat, n_p, seq, lam_row, att_subln[l], lam_init, tq=512)
        yc_s = _attn_sample(a_mat, tp, n_s, dec_seq, cache_k, cache_v, page_table, lam_row,
                            att_subln[l], lam_init, l, pages_per_step=8)

        z = jnp.concatenate([z_p, z_s], axis=0)
        yb = jnp.concatenate([yb_p, yb_s], axis=0)
        yc = jnp.concatenate([yc_p, yc_s], axis=0)
        merged = _merge(z, yb, yc, gates, w_up_ssm, w_up_pool, w_up_att, l, tm=tm, tn=512)
        x = _matmul(merged, w_out, l, res=x, tm=tm, tn=512, name="w_out")

        mk_p = _matmul(mem_rows, w_mem_k, l, gain=norm_mem[l], tm=n_p * N_MEM, tn=512, name="mem_k")
        mv_p = _matmul(mem_rows, w_mem_v, l, gain=norm_mem[l], tm=n_p * N_MEM, tn=512, name="mem_v")
        outs["mkp"].append(mk_p.reshape(n_p, N_MEM, N_HEADS_MEM, HEAD_DIM_MEM))
        outs["mvp"].append(mv_p.reshape(n_p, N_MEM, N_HEADS_MEM, HEAD_DIM_MEM))
        qm = _matmul(x, w_mem_q, l, gain=norm_cross[l], tm=tm, tn=512, name="mem_q")
        ca_p = _cross_attn(qm, 0, n_p, seq, mk_p.reshape(n_p, N_MEM, d), mv_p.reshape(n_p, N_MEM, d),
                           lambda n: (n, 0, 0), tq=512)
        ca_s = _cross_attn(qm, tp, n_s, dec_seq, cmk, cmv, lambda n: (l, n, 0, 0), tq=dec_seq)
        x = _matmul(jnp.concatenate([ca_p, ca_s], axis=0), w_mem_o, l, res=x, tm=tm, tn=512, name="mem_o")

        xn, *score = _peer_score(x, norm_ffn[l], peer_w_query, peer_keys1, peer_keys2, l, tm=tm)
        x = _peer_dense(xn, x, score, peer_u, peer_v, l, tm=384, te=1024)

    y = _final_norm(x, norm_final, tm=tm)
    st = lambda k: jnp.stack(outs[k])
    return (y[:tp].reshape(n_p, seq, d), y[tp:].reshape(n_s, dec_seq, d),
            st("kp"), st("vp"), st("mkp"), st("mvp"), st("hrp"), st("hip"), st("bp"),
            st("ks"), st("vs"), st("hrs"), st("his"), st("bs"))
```

```python
import functools
import math

import jax
import jax.numpy as jnp
from jax import lax
from jax.experimental import pallas as pl
from jax.experimental.pallas import tpu as pltpu

F32 = jnp.float32
BF16 = jnp.bfloat16
EPS = 1e-6
NEG_INF = -1e30
LOG2E = math.log2(math.e)

D_MODEL = 2048
W_SSM = 512
SSM_GROUP = 16
N_SSM_GROUPS = 32
SSM_STATE = 64
N_STATE = N_SSM_GROUPS * SSM_STATE
W_POOL = 512
POOL_WINDOWS = (2, 4, 8, 16)
POOL_GROUP = 128
POOL_BUF = 15
POOL_CARRY = 16
N_HEADS_ATT = 8
HEAD_DIM_ATT = 64
W_HEAD = 2 * HEAD_DIM_ATT
W_ATT = N_HEADS_ATT * W_HEAD
N_MEM = 256
N_HEADS_MEM = 4
HEAD_DIM_MEM = 512
N_KEYS = 128
N_RETR_HEADS = 8
D_QUERY = 256
TOPK = 16

COL_SSM = 0
COL_POOL = W_SSM
COL_Q = W_SSM + W_POOL
COL_K = COL_Q + W_ATT
COL_V = COL_K + W_ATT
COL_GATE = COL_V + W_ATT
IN_COLS = COL_GATE + 3 * D_MODEL

VMEM_LIMIT_BYTES = 56 * 1024 * 1024
NT_DIMS = (((1,), (1,)), ((), ()))


def _cparams(*sem):
    return pltpu.CompilerParams(dimension_semantics=sem, vmem_limit_bytes=VMEM_LIMIT_BYTES)


def _bdot(a, b):
    return jnp.dot(a.astype(BF16), b.astype(BF16), preferred_element_type=F32)


def _bdot_nt(a, b):
    return lax.dot_general(a.astype(BF16), b.astype(BF16), NT_DIMS, preferred_element_type=F32)


def _rms(x, g):
    return x * lax.rsqrt(jnp.mean(x * x, axis=-1, keepdims=True) + EPS) * g


def _gelu(x):
    return 0.5 * x * (1.0 + lax.erf(x * (2.0 ** -0.5)))


def _in_proj_kernel(x_ref, g_ref, w_ref, o_ref, xn_ref, *, gate):
    @pl.when(pl.program_id(1) == 0)
    def _():
        xn_ref[...] = _rms(x_ref[...], g_ref[...]).astype(BF16)

    res = jnp.dot(xn_ref[...], w_ref[...], preferred_element_type=F32)
    o_ref[...] = jax.nn.sigmoid(res) if gate else res


def _in_proj(x, g, w_in, l, col0, n_cols, *, gate, tm, tn):
    t = x.shape[0]
    jb0 = col0 // tn
    return pl.pallas_call(
        functools.partial(_in_proj_kernel, gate=gate),
        grid=(t // tm, n_cols // tn),
        in_specs=[
            pl.BlockSpec((tm, D_MODEL), lambda i, j: (i, 0)),
            pl.BlockSpec((1, D_MODEL), lambda i, j: (0, 0)),
            pl.BlockSpec((None, D_MODEL, tn), lambda i, j: (l, 0, jb0 + j)),
        ],
        out_specs=pl.BlockSpec((tm, tn), lambda i, j: (i, j)),
        out_shape=jax.ShapeDtypeStruct((t, n_cols), F32),
        scratch_shapes=[pltpu.VMEM((tm, D_MODEL), BF16)],
        compiler_params=_cparams("parallel", "arbitrary"),
        name="in_gate" if gate else "in_proj",
    )(x, g.reshape(1, D_MODEL), w_in)


def _mm_kernel(*refs, has_norm, has_res):
    x_ref = refs[0]
    pos = 1
    g_ref = None
    if has_norm:
        g_ref = refs[pos]
        pos += 1
    w_ref = refs[pos]
    pos += 1
    r_ref = None
    if has_res:
        r_ref = refs[pos]
        pos += 1
    o_ref = refs[pos]
    x = x_ref[...]
    if has_norm:
        x = _rms(x, g_ref[...])
    res = jnp.dot(x.astype(BF16), w_ref[...], preferred_element_type=F32)
    if has_res:
        res = res + r_ref[...]
    o_ref[...] = res


def _matmul(x, w, l, *, gain=None, res=None, tm, name):
    t, k = x.shape
    n = w.shape[2]
    has_norm = gain is not None
    has_res = res is not None
    in_specs = [pl.BlockSpec((tm, k), lambda i: (i, 0))]
    args = [x]
    if has_norm:
        in_specs.append(pl.BlockSpec((1, k), lambda i: (0, 0)))
        args.append(gain.reshape(1, k))
    in_specs.append(pl.BlockSpec((None, k, n), lambda i: (l, 0, 0), pipeline_mode=pl.Buffered(1)))
    args.append(w)
    if has_res:
        in_specs.append(pl.BlockSpec((tm, n), lambda i: (i, 0)))
        args.append(res)
    return pl.pallas_call(
        functools.partial(_mm_kernel, has_norm=has_norm, has_res=has_res),
        grid=(t // tm,),
        in_specs=in_specs,
        out_specs=pl.BlockSpec((tm, n), lambda i: (i, 0)),
        out_shape=jax.ShapeDtypeStruct((t, n), F32),
        compiler_params=_cparams("parallel"),
        name=name,
    )(*args)


def _merge_kernel(z_ref, yb_ref, yc_ref, g0_ref, g1_ref, g2_ref, wa_ref, wb_ref, wc_ref, o_ref):
    o_ref[...] = (g0_ref[...] * _bdot(z_ref[...], wa_ref[...])
                  + g1_ref[...] * _bdot(yb_ref[...], wb_ref[...])
                  + g2_ref[...] * _bdot(yc_ref[...], wc_ref[...]))


def _merge(z, yb, yc, gates, w_up_ssm, w_up_pool, w_up_att, l, *, tm, tn):
    t = z.shape[0]
    nb = D_MODEL // tn
    return pl.pallas_call(
        _merge_kernel,
        grid=(t // tm, nb),
        in_specs=[
            pl.BlockSpec((tm, W_SSM), lambda i, j: (i, 0)),
            pl.BlockSpec((tm, W_POOL), lambda i, j: (i, 0)),
            pl.BlockSpec((tm, W_ATT), lambda i, j: (i, 0)),
            pl.BlockSpec((tm, tn), lambda i, j: (i, j)),
            pl.BlockSpec((tm, tn), lambda i, j: (i, nb + j)),
            pl.BlockSpec((tm, tn), lambda i, j: (i, 2 * nb + j)),
            pl.BlockSpec((None, W_SSM, tn), lambda i, j: (l, 0, j)),
            pl.BlockSpec((None, W_POOL, tn), lambda i, j: (l, 0, j)),
            pl.BlockSpec((None, W_ATT, tn), lambda i, j: (l, 0, j)),
        ],
        out_specs=pl.BlockSpec((tm, tn), lambda i, j: (i, j)),
        out_shape=jax.ShapeDtypeStruct((t, D_MODEL), F32),
        compiler_params=_cparams("parallel", "arbitrary"),
        name="merge_up",
    )(z, yb, yc, gates, gates, gates, w_up_ssm, w_up_pool, w_up_att)


def _final_norm_kernel(x_ref, g_ref, o_ref):
    o_ref[...] = _rms(x_ref[...], g_ref[...])


def _final_norm(x, g, *, tm):
    t = x.shape[0]
    return pl.pallas_call(
        _final_norm_kernel,
        grid=(t // tm,),
        in_specs=[pl.BlockSpec((tm, D_MODEL), lambda i: (i, 0)),
                  pl.BlockSpec((1, D_MODEL), lambda i: (0, 0))],
        out_specs=pl.BlockSpec((tm, D_MODEL), lambda i: (i, 0)),
        out_shape=jax.ShapeDtypeStruct((t, D_MODEL), F32),
        compiler_params=_cparams("parallel"),
        name="final_norm",
    )(x, g.reshape(1, D_MODEL))


def _s5_tables(a_re, a_im, log_dt, b_re, b_im, c_re, c_im, tc):
    g_, p_ = N_SSM_GROUPS, SSM_STATE
    dt = jnp.exp(log_dt)[:, None]
    mag = jnp.exp(a_re * dt)
    lb_re = mag * jnp.cos(a_im * dt)
    lb_im = mag * jnp.sin(a_im * dt)
    den = a_re * a_re + a_im * a_im
    nr = lb_re - 1.0
    f_re = (nr * a_re + lb_im * a_im) / den
    f_im = (lb_im * a_re - nr * a_im) / den
    bb_re = f_re[..., None] * b_re - f_im[..., None] * b_im
    bb_im = f_re[..., None] * b_im + f_im[..., None] * b_re
    eye = jnp.eye(g_, dtype=F32)
    bd_re = jnp.einsum('gpc,gh->gchp', bb_re, eye).reshape(W_SSM, N_STATE)
    bd_im = jnp.einsum('gpc,gh->gchp', bb_im, eye).reshape(W_SSM, N_STATE)
    bmat = jnp.concatenate([bd_re, bd_im], axis=1).astype(BF16)
    cd_re = jnp.einsum('gcp,gh->gphc', c_re, eye).reshape(N_STATE, W_SSM).astype(BF16)
    cd_im = jnp.einsum('gcp,gh->gphc', -c_im, eye).reshape(N_STATE, W_SSM).astype(BF16)
    lr = lb_re.reshape(1, N_STATE)
    li = lb_im.reshape(1, N_STATE)
    nsteps = max(1, int(math.log2(tc)))
    sq_re, sq_im = [lr], [li]
    for _ in range(nsteps):
        r, i = sq_re[-1], sq_im[-1]
        sq_re.append(r * r - i * i)
        sq_im.append(2.0 * r * i)
    pt_re, pt_im = lr, li
    for s in range(nsteps):
        r, i = sq_re[s], sq_im[s]
        pt_re, pt_im = (jnp.concatenate([pt_re, pt_re * r - pt_im * i], axis=0),
                        jnp.concatenate([pt_im, pt_re * i + pt_im * r], axis=0))
    pw_re = jnp.concatenate(sq_re[:nsteps], axis=0)
    pw_im = jnp.concatenate(sq_im[:nsteps], axis=0)
    return bmat, cd_re, cd_im, pw_re, pw_im, pt_re[:tc], pt_im[:tc]


def _s5_kernel(u_ref, b_ref, cre_ref, cim_ref, d_ref, pwr_ref, pwi_ref, ptr_ref, pti_ref,
               h0r_ref, h0i_ref, wglu_ref, z_ref, hr_ref, hi_ref, cr_ref, ci_ref, *, tc, nsteps):
    c = pl.program_id(1)

    @pl.when(c == 0)
    def _():
        cr_ref[...] = h0r_ref[...]
        ci_ref[...] = h0i_ref[...]

    u = u_ref[...]
    bu = jnp.dot(u.astype(BF16), b_ref[...], preferred_element_type=F32)
    re = bu[:, :N_STATE]
    im = bu[:, N_STATE:]
    row = lax.broadcasted_iota(jnp.int32, (tc, 1), 0)
    for s in range(nsteps):
        k = 1 << s
        keep = row >= k
        sre = jnp.where(keep, pltpu.roll(re, k, 0), 0.0)
        sim = jnp.where(keep, pltpu.roll(im, k, 0), 0.0)
        ar = pwr_ref[s:s + 1, :]
        ai = pwi_ref[s:s + 1, :]
        re, im = re + (ar * sre - ai * sim), im + (ar * sim + ai * sre)
    hr = cr_ref[...]
    hi = ci_ref[...]
    pr = ptr_ref[...]
    pi = pti_ref[...]
    re = re + (pr * hr - pi * hi)
    im = im + (pr * hi + pi * hr)
    cr_ref[...] = re[tc - 1:tc, :]
    ci_ref[...] = im[tc - 1:tc, :]
    y = (jnp.dot(re.astype(BF16), cre_ref[...], preferred_element_type=F32)
         + jnp.dot(im.astype(BF16), cim_ref[...], preferred_element_type=F32)
         + d_ref[...] * u)
    z = _gelu(y)
    z_ref[...] = z * jax.nn.sigmoid(_bdot(z, wglu_ref[...]))

    @pl.when(c == pl.num_programs(1) - 1)
    def _():
        hr_ref[...] = re[tc - 1:tc, :]
        hi_ref[...] = im[tc - 1:tc, :]


def _s5(a_mat, row0, n_seq, seq_len, tables, d_skip, h0_re, h0_im, w_glu, l, *, tc):
    bmat, cd_re, cd_im, pw_re, pw_im, pt_re, pt_im = tables
    nsteps = pw_re.shape[0] if tc > 1 else 0
    nc = seq_len // tc
    rb0 = row0 // tc
    const = lambda shape: pl.BlockSpec(shape, lambda n, c: tuple(0 for _ in shape))
    z, hr, hi = pl.pallas_call(
        functools.partial(_s5_kernel, tc=tc, nsteps=nsteps),
        grid=(n_seq, nc),
        in_specs=[
            pl.BlockSpec((tc, W_SSM), lambda n, c: (rb0 + n * nc + c, COL_SSM // W_SSM)),
            const((W_SSM, 2 * N_STATE)),
            const((N_STATE, W_SSM)),
            const((N_STATE, W_SSM)),
            const((1, W_SSM)),
            const(pw_re.shape),
            const(pw_im.shape),
            const((tc, N_STATE)),
            const((tc, N_STATE)),
            pl.BlockSpec((None, 1, N_STATE), lambda n, c: (n, 0, 0)),
            pl.BlockSpec((None, 1, N_STATE), lambda n, c: (n, 0, 0)),
            pl.BlockSpec((None, W_SSM, W_SSM), lambda n, c: (l, 0, 0)),
        ],
        out_specs=[
            pl.BlockSpec((tc, W_SSM), lambda n, c: (n * nc + c, 0)),
            pl.BlockSpec((None, 1, N_STATE), lambda n, c: (n, 0, 0)),
            pl.BlockSpec((None, 1, N_STATE), lambda n, c: (n, 0, 0)),
        ],
        out_shape=[jax.ShapeDtypeStruct((n_seq * seq_len, W_SSM), F32),
                   jax.ShapeDtypeStruct((n_seq, 1, N_STATE), F32),
                   jax.ShapeDtypeStruct((n_seq, 1, N_STATE), F32)],
        scratch_shapes=[pltpu.VMEM((1, N_STATE), F32), pltpu.VMEM((1, N_STATE), F32)],
        compiler_params=_cparams("parallel", "arbitrary"),
        name="s5",
    )(a_mat, bmat, cd_re, cd_im, d_skip.reshape(1, W_SSM), pw_re, pw_im, pt_re, pt_im,
      h0_re.reshape(n_seq, 1, N_STATE), h0_im.reshape(n_seq, 1, N_STATE), w_glu)
    shp = (n_seq, N_SSM_GROUPS, SSM_STATE)
    return z, hr.reshape(shp), hi.reshape(shp)


def _pool_kernel(u_ref, buf_ref, w_ref, sc_ref, y_ref, nb_ref, xx_ref, *, tc, start_pos):
    c = pl.program_id(1)

    @pl.when(c == 0)
    def _():
        xx_ref[0:POOL_CARRY, :] = buf_ref[...]

    u = u_ref[...]
    xx_ref[POOL_CARRY:, :] = u
    x = xx_ref[...]
    sums = []
    s = x
    for w in (1, 2, 4, 8):
        s = s + pltpu.roll(s, w, 0)
        sums.append(s)
    t_abs = (start_pos + c * tc + lax.broadcasted_iota(jnp.int32, (tc, 1), 0) + 1).astype(F32)
    outs = []
    for gi, w in enumerate(POOL_WINDOWS):
        sl = slice(gi * POOL_GROUP, (gi + 1) * POOL_GROUP)
        mean = sums[gi][POOL_CARRY:, sl] / jnp.minimum(t_abs, float(w))
        d = mean - u[:, sl]
        outs.append(_bdot(d, w_ref[gi]))
    y_ref[...] = jnp.concatenate(outs, axis=-1) * sc_ref[...]
    carry = x[tc:, :]
    xx_ref[0:POOL_CARRY, :] = carry

    @pl.when(c == pl.num_programs(1) - 1)
    def _():
        nb_ref[...] = carry


def _pool(a_mat, row0, n_seq, seq_len, buf16, pool_w, pool_scale, l, start_pos, *, tc):
    nc = seq_len // tc
    rb0 = row0 // tc
    y, nb = pl.pallas_call(
        functools.partial(_pool_kernel, tc=tc, start_pos=start_pos),
        grid=(n_seq, nc),
        in_specs=[
            pl.BlockSpec((tc, W_POOL), lambda n, c: (rb0 + n * nc + c, COL_POOL // W_POOL)),
            pl.BlockSpec((None, POOL_CARRY, W_POOL), lambda n, c: (n, 0, 0)),
            pl.BlockSpec((None, len(POOL_WINDOWS), POOL_GROUP, POOL_GROUP), lambda n, c: (l, 0, 0, 0)),
            pl.BlockSpec((1, W_POOL), lambda n, c: (0, 0)),
        ],
        out_specs=[
            pl.BlockSpec((tc, W_POOL), lambda n, c: (n * nc + c, 0)),
            pl.BlockSpec((None, POOL_CARRY, W_POOL), lambda n, c: (n, 0, 0)),
        ],
        out_shape=[jax.ShapeDtypeStruct((n_seq * seq_len, W_POOL), F32),
                   jax.ShapeDtypeStruct((n_seq, POOL_CARRY, W_POOL), F32)],
        scratch_shapes=[pltpu.VMEM((POOL_CARRY + tc, W_POOL), F32)],
        compiler_params=_cparams("parallel", "arbitrary"),
        name="pool",
    )(a_mat, buf16, pool_w, pool_scale.reshape(1, W_POOL))
    return y, nb[:, POOL_CARRY - POOL_BUF:, :]


def _subln(o, g, lam_init):
    return _rms(o, g) * (1.0 - lam_init)


def _attn_prompt_kernel(qt_ref, kt_ref, q_ref, k_ref, v_ref, lam_ref, g_ref, o_ref,
                        m_ref, l_ref, acc_ref, *, tq, hb, lam_init):
    p = pl.program_id(2)
    qi = qt_ref[p]
    ki = kt_ref[p]

    @pl.when(ki == 0)
    def _():
        m_ref[...] = jnp.full(m_ref.shape, NEG_INF, F32)
        l_ref[...] = jnp.zeros(l_ref.shape, F32)
        acc_ref[...] = jnp.zeros(acc_ref.shape, F32)

    def accumulate(masked):
        q = (q_ref[...] * (HEAD_DIM_ATT ** -0.5 * LOG2E)).astype(BF16)
        k = k_ref[...].astype(BF16)
        v = v_ref[...].astype(BF16)
        if masked:
            causal = (lax.broadcasted_iota(jnp.int32, (tq, tq), 0)
                      >= lax.broadcasted_iota(jnp.int32, (tq, tq), 1))
        for hh in range(hb):
            vh = v[:, hh * W_HEAD:(hh + 1) * W_HEAD]
            for c in range(2):
                idx = 2 * hh + c
                sl = slice(hh * W_HEAD + c * HEAD_DIM_ATT, hh * W_HEAD + (c + 1) * HEAD_DIM_ATT)
                s = lax.dot_general(q[:, sl], k[:, sl], NT_DIMS, preferred_element_type=F32)
                if masked:
                    s = jnp.where(causal, s, NEG_INF)
                m_prev = m_ref[idx]
                m_new = jnp.maximum(m_prev, jnp.max(s, axis=-1, keepdims=True))
                alpha = jnp.exp2(m_prev - m_new)
                pexp = jnp.exp2(s - jnp.concatenate([m_new] * (tq // W_HEAD), axis=1))
                l_ref[idx] = alpha * l_ref[idx] + jnp.sum(pexp, axis=-1, keepdims=True)
                acc_ref[idx] = alpha * acc_ref[idx] + jnp.dot(pexp.astype(BF16), vh,
                                                              preferred_element_type=F32)
                m_ref[idx] = m_new

    @pl.when(ki < qi)
    def _():
        accumulate(False)

    @pl.when(ki == qi)
    def _():
        accumulate(True)
        lam = lam_ref[0:1, :]
        for hh in range(hb):
            o = (acc_ref[2 * hh] / l_ref[2 * hh]
                 - lam * (acc_ref[2 * hh + 1] / l_ref[2 * hh + 1]))
            o_ref[:, hh * W_HEAD:(hh + 1) * W_HEAD] = _subln(o, g_ref[...], lam_init)


def _attn_prompt(a_mat, n_seq, seq_len, lam_row, subln_g, lam_init, *, tq, hb):
    nq = seq_len // tq
    pairs = [(qi, ki) for qi in range(nq) for ki in range(qi + 1)]
    qt = jnp.asarray([p[0] for p in pairs], jnp.int32)
    kt = jnp.asarray([p[1] for p in pairs], jnp.int32)
    wb = hb * W_HEAD
    cq, ck, cv = COL_Q // wb, COL_K // wb, COL_V // wb
    grid_spec = pltpu.PrefetchScalarGridSpec(
        num_scalar_prefetch=2,
        grid=(n_seq, N_HEADS_ATT // hb, len(pairs)),
        in_specs=[
            pl.BlockSpec((tq, wb), lambda b, h, p, qt, kt: (b * nq + qt[p], cq + h)),
            pl.BlockSpec((tq, wb), lambda b, h, p, qt, kt: (b * nq + kt[p], ck + h)),
            pl.BlockSpec((tq, wb), lambda b, h, p, qt, kt: (b * nq + kt[p], cv + h)),
            pl.BlockSpec((8, W_HEAD), lambda b, h, p, qt, kt: (0, 0)),
            pl.BlockSpec((1, W_HEAD), lambda b, h, p, qt, kt: (0, 0)),
        ],
        out_specs=pl.BlockSpec((tq, wb), lambda b, h, p, qt, kt: (b * nq + qt[p], h)),
        scratch_shapes=[pltpu.VMEM((2 * hb, tq, W_HEAD), F32), pltpu.VMEM((2 * hb, tq, W_HEAD), F32),
                        pltpu.VMEM((2 * hb, tq, W_HEAD), F32)],
    )
    return pl.pallas_call(
        functools.partial(_attn_prompt_kernel, tq=tq, hb=hb, lam_init=lam_init),
        grid_spec=grid_spec,
        out_shape=jax.ShapeDtypeStruct((n_seq * seq_len, W_ATT), F32),
        compiler_params=_cparams("parallel", "parallel", "arbitrary"),
        name="attn_prompt",
    )(qt, kt, a_mat, a_mat, a_mat, lam_row, subln_g.reshape(1, W_HEAD))


def _attn_sample_kernel(pt_ref, q_ref, kn_ref, vn_ref, lam_ref, g_ref, *rest, n_pages, dec_seq, lam_init):
    k_refs = rest[:n_pages]
    v_refs = rest[n_pages:2 * n_pages]
    o_ref, qh_ref, m_ref, l_ref, acc_ref = rest[2 * n_pages:]
    step = pl.program_id(1)
    rph = 2 * dec_seq
    page = k_refs[0].shape[0] // N_HEADS_ATT

    @pl.when(step == 0)
    def _():
        q = q_ref[...] * (HEAD_DIM_ATT ** -0.5)
        row = lax.broadcasted_iota(jnp.int32, (rph, W_HEAD), 0)
        lane = lax.broadcasted_iota(jnp.int32, (rph, W_HEAD), 1)
        keep = lane // HEAD_DIM_ATT == row // dec_seq
        for h in range(N_HEADS_ATT):
            qh = q[:, h * W_HEAD:(h + 1) * W_HEAD]
            qh_ref[h] = jnp.where(keep, jnp.concatenate([qh, qh], axis=0), 0.0).astype(BF16)
        m_ref[...] = jnp.full(m_ref.shape, NEG_INF, F32)
        l_ref[...] = jnp.zeros(l_ref.shape, F32)
        acc_ref[...] = jnp.zeros(acc_ref.shape, F32)

    def head_rows(refs, h):
        return jnp.concatenate([r[pl.ds(h, page, stride=N_HEADS_ATT), :] for r in refs],
                               axis=0).astype(BF16)

    def update(s, v_of_head):
        m_prev = m_ref[...]
        m_new = jnp.maximum(m_prev, jnp.max(s, axis=-1, keepdims=True))
        alpha = jnp.exp(m_prev - m_new)
        pexp = jnp.exp(s - m_new)
        l_ref[...] = alpha * l_ref[...] + jnp.sum(pexp, axis=-1, keepdims=True)
        pb = pexp.astype(BF16)
        pv = jnp.concatenate(
            [jnp.dot(pb[h * rph:(h + 1) * rph], v_of_head(h), preferred_element_type=F32)
             for h in range(N_HEADS_ATT)], axis=0)
        acc_ref[...] = alpha * acc_ref[...] + pv
        m_ref[...] = m_new

    s = jnp.concatenate(
        [lax.dot_general(qh_ref[h], head_rows(k_refs, h), NT_DIMS, preferred_element_type=F32)
         for h in range(N_HEADS_ATT)], axis=0)
    update(s, lambda h: head_rows(v_refs, h))

    @pl.when(step == pl.num_programs(1) - 1)
    def _():
        pad = jnp.zeros((W_HEAD - dec_seq, W_HEAD), F32)

        def new_rows(ref, h):
            return jnp.concatenate([ref[:, h * W_HEAD:(h + 1) * W_HEAD], pad], axis=0).astype(BF16)

        s2 = jnp.concatenate(
            [lax.dot_general(qh_ref[h], new_rows(kn_ref, h), NT_DIMS, preferred_element_type=F32)
             for h in range(N_HEADS_ATT)], axis=0)
        qi = lax.broadcasted_iota(jnp.int32, s2.shape, 0) % dec_seq
        kj = lax.broadcasted_iota(jnp.int32, s2.shape, 1)
        s2 = jnp.where(qi >= kj, s2, NEG_INF)
        update(s2, lambda h: new_rows(vn_ref, h))
        o = acc_ref[...] / l_ref[...]
        lam = lam_ref[0:1, :]
        for h in range(N_HEADS_ATT):
            r0 = h * rph
            oh = o[r0:r0 + dec_seq] - lam * o[r0 + dec_seq:r0 + rph]
            o_ref[:, h * W_HEAD:(h + 1) * W_HEAD] = _subln(oh, g_ref[...], lam_init)


def _attn_sample(a_mat, row0, n_seq, dec_seq, cache_k, cache_v, page_table, lam_row, subln_g,
                 lam_init, l, *, pages_per_step):
    depth, n_pool, page, _, _ = cache_k.shape
    ck = cache_k.reshape(depth, n_pool, page * N_HEADS_ATT, W_HEAD)
    cv = cache_v.reshape(depth, n_pool, page * N_HEADS_ATT, W_HEAD)
    n_pt = page_table.shape[1]
    r = pages_per_step
    rb0 = row0 // dec_seq
    cq, ckk, cvv = COL_Q // W_ATT, COL_K // W_ATT, COL_V // W_ATT

    def page_spec(j):
        return pl.BlockSpec((None, None, page * N_HEADS_ATT, W_HEAD),
                            lambda n, s, pt: (l, pt[n, s * r + j], 0, 0))

    grid_spec = pltpu.PrefetchScalarGridSpec(
        num_scalar_prefetch=1,
        grid=(n_seq, n_pt // r),
        in_specs=[
            pl.BlockSpec((dec_seq, W_ATT), lambda n, s, pt: (rb0 + n, cq)),
            pl.BlockSpec((dec_seq, W_ATT), lambda n, s, pt: (rb0 + n, ckk)),
            pl.BlockSpec((dec_seq, W_ATT), lambda n, s, pt: (rb0 + n, cvv)),
            pl.BlockSpec((8, W_HEAD), lambda n, s, pt: (0, 0)),
            pl.BlockSpec((1, W_HEAD), lambda n, s, pt: (0, 0)),
        ] + [page_spec(j) for j in range(r)] + [page_spec(j) for j in range(r)],
        out_specs=pl.BlockSpec((dec_seq, W_ATT), lambda n, s, pt: (n, 0)),
        scratch_shapes=[pltpu.VMEM((N_HEADS_ATT, 2 * dec_seq, W_HEAD), BF16),
                        pltpu.VMEM((2 * N_HEADS_ATT * dec_seq, 1), F32),
                        pltpu.VMEM((2 * N_HEADS_ATT * dec_seq, 1), F32),
                        pltpu.VMEM((2 * N_HEADS_ATT * dec_seq, W_HEAD), F32)],
    )
    return pl.pallas_call(
        functools.partial(_attn_sample_kernel, n_pages=r, dec_seq=dec_seq, lam_init=lam_init),
        grid_spec=grid_spec,
        out_shape=jax.ShapeDtypeStruct((n_seq * dec_seq, W_ATT), F32),
        compiler_params=_cparams("parallel", "arbitrary"),
        name="attn_sample",
    )(page_table, a_mat, a_mat, a_mat, lam_row, subln_g.reshape(1, W_HEAD),
      *([ck] * r), *([cv] * r))


def _cross_kernel(q_ref, k_ref, v_ref, o_ref):
    q = q_ref[...]
    scale = HEAD_DIM_MEM ** -0.5
    for h in range(N_HEADS_MEM):
        sl = slice(h * HEAD_DIM_MEM, (h + 1) * HEAD_DIM_MEM)
        s = _bdot_nt(q[:, sl], k_ref[:, sl]) * scale
        s = s - jnp.max(s, axis=-1, keepdims=True)
        e = jnp.exp(s)
        pr = e / jnp.sum(e, axis=-1, keepdims=True)
        o_ref[:, sl] = _bdot(pr, v_ref[:, sl])


def _cross_attn(qm, row0, n_seq, seq_len, mem_k, mem_v, kv_index, *, tq):
    nq = seq_len // tq
    rb0 = row0 // tq
    blk = tuple(None for _ in range(mem_k.ndim - 2)) + (N_MEM, D_MODEL)
    return pl.pallas_call(
        _cross_kernel,
        grid=(n_seq, nq),
        in_specs=[
            pl.BlockSpec((tq, D_MODEL), lambda n, i: (rb0 + n * nq + i, 0)),
            pl.BlockSpec(blk, lambda n, i: kv_index(n)),
            pl.BlockSpec(blk, lambda n, i: kv_index(n)),
        ],
        out_specs=pl.BlockSpec((tq, D_MODEL), lambda n, i: (n * nq + i, 0)),
        out_shape=jax.ShapeDtypeStruct((n_seq * seq_len, D_MODEL), F32),
        compiler_params=_cparams("parallel", "arbitrary"),
        name="cross_attn",
    )(qm, mem_k, mem_v)


def _oddeven_sort_pairs(n):
    pairs = []

    def merge(lo, hi, r):
        step = r * 2
        if step < hi - lo:
            merge(lo, hi, step)
            merge(lo + r, hi, step)
            pairs.extend((i, i + r) for i in range(lo + r, hi - r, step))
        else:
            pairs.append((lo, lo + r))

    def sort(lo, hi):
        if hi - lo >= 1:
            mid = lo + (hi - lo) // 2
            sort(lo, mid)
            sort(mid + 1, hi)
            merge(lo, hi, 1)

    sort(0, n - 1)
    return pairs


_SORT16 = _oddeven_sort_pairs(TOPK)


def _cmpx(xs, i, j):
    a, b = xs[i], xs[j]
    if a is None:
        xs[i], xs[j] = b, None
    elif b is not None:
        xs[i], xs[j] = jnp.maximum(a, b), jnp.minimum(a, b)


def _sort16_desc(xs):
    xs = list(xs)
    for i, j in _SORT16:
        _cmpx(xs, i, j)
    return xs


def _merge_top16(a, b):
    c = []
    for i in range(TOPK):
        x, y = a[i], b[TOPK - 1 - i]
        c.append(y if x is None else (x if y is None else jnp.maximum(x, y)))
    stride = TOPK // 2
    while stride:
        for i in range(TOPK):
            if not i & stride:
                _cmpx(c, i, i + stride)
        stride //= 2
    return c


def _top16_rows(st):
    lists = _sort16_desc([st[8 * v:8 * v + 8, :] for v in range(N_KEYS // 8)])
    for shift in (4, 2, 1):
        lists = _merge_top16(lists, [pltpu.roll(x, shift, 0) for x in lists])
    return lists


_CAND_PAIRS = [(i, j) for i in range(TOPK) for j in range(TOPK) if (i + 1) * (j + 1) <= TOPK]


def _peer_score_kernel(x_ref, g_ref, wq_ref, k1_ref, k2_ref,
                       xn_ref, s1_ref, p1_ref, s2_ref, p2_ref, tau_ref):
    h = pl.program_id(1)

    @pl.when(h == 0)
    def _():
        xn_ref[...] = _rms(x_ref[...], g_ref[...]).astype(BF16)

    q = jnp.dot(xn_ref[...], wq_ref[...].astype(BF16), preferred_element_type=F32)
    half = D_QUERY // 2
    s1 = _bdot_nt(k1_ref[...], q[:, :half])
    s2 = _bdot_nt(k2_ref[...], q[:, half:])
    v1 = _top16_rows(s1)
    v2 = _top16_rows(s2)
    cands = [v1[i] + v2[j] for i, j in _CAND_PAIRS]
    cands += [None] * (-len(cands) % TOPK)
    groups = [_sort16_desc(cands[g:g + TOPK]) for g in range(0, len(cands), TOPK)]
    top = groups[0]
    for grp in groups[1:]:
        top = _merge_top16(top, grp)
    z = jnp.exp(top[0] - top[0])
    for t in top[1:]:
        z = z + jnp.exp(t - top[0])
    s1_ref[...] = s1
    s2_ref[...] = s2
    p1_ref[...] = jnp.exp(s1 - v1[0][0:1, :]) / z[0:1, :]
    p2_ref[...] = jnp.exp(s2 - v2[0][0:1, :])
    tau_ref[...] = top[TOPK - 1]


def _peer_score(x, g, w_query, keys1, keys2, l, *, tm):
    t = x.shape[0]
    nh = N_RETR_HEADS
    key_spec = pl.BlockSpec((None, None, N_KEYS, D_QUERY // 2), lambda i, h: (l, h, 0, 0))
    st_spec = pl.BlockSpec((None, N_KEYS, tm), lambda i, h: (h, 0, i))
    st_shape = jax.ShapeDtypeStruct((nh, N_KEYS, t), F32)
    return pl.pallas_call(
        _peer_score_kernel,
        grid=(t // tm, nh),
        in_specs=[
            pl.BlockSpec((tm, D_MODEL), lambda i, h: (i, 0)),
            pl.BlockSpec((1, D_MODEL), lambda i, h: (0, 0)),
            pl.BlockSpec((None, D_MODEL, D_QUERY), lambda i, h: (l, 0, h)),
            key_spec, key_spec,
        ],
        out_specs=[
            pl.BlockSpec((tm, D_MODEL), lambda i, h: (i, 0)),
            st_spec, st_spec, st_spec, st_spec,
            pl.BlockSpec((None, 8, tm), lambda i, h: (h, 0, i)),
        ],
        out_shape=[jax.ShapeDtypeStruct((t, D_MODEL), BF16),
                   st_shape, st_shape, st_shape, st_shape,
                   jax.ShapeDtypeStruct((nh, 8, t), F32)],
        compiler_params=_cparams("parallel", "arbitrary"),
        name="peer_score",
    )(x, g.reshape(1, D_MODEL), w_query, keys1, keys2)


def _peer_dense_kernel(xn_ref, u_ref, v_ref, s1_ref, p1_ref, s2_ref, p2_ref, tau_ref, x_ref,
                       o_ref, *, n_a, a_blk):
    e = pl.program_id(1)

    @pl.when(e == 0)
    def _():
        o_ref[...] = x_ref[...]

    a0 = (e % (a_blk // n_a)) * n_a
    s1_rows = [[s1_ref[h, pl.ds(a0 + al, 1), :] for h in range(N_RETR_HEADS)] for al in range(n_a)]
    p1_rows = [[p1_ref[h, pl.ds(a0 + al, 1), :] for h in range(N_RETR_HEADS)] for al in range(n_a)]
    ht = _bdot_nt(u_ref[...], xn_ref[...])
    blocks = []
    for al in range(n_a):
        row = []
        for c in range(0, xn_ref.shape[0], N_KEYS):
            tok = slice(c, c + N_KEYS)
            g = None
            for h in range(N_RETR_HEADS):
                s = s1_rows[al][h][:, tok] + s2_ref[h, :, tok]
                w = p1_rows[al][h][:, tok] * p2_ref[h, :, tok]
                piece = jnp.where(s >= tau_ref[h, 0:1, tok], w, 0.0)
                g = piece if g is None else g + piece
            row.append(g)
        blocks.append(jnp.concatenate(row, axis=1))
    wt = jnp.concatenate(blocks, axis=0) * _gelu(ht)
    o_ref[...] += jnp.dot(wt.T.astype(BF16), v_ref[...], preferred_element_type=F32)


def _peer_dense(xn, x_res, score, peer_u, peer_v, l, *, tm, te):
    s1t, p1t, s2t, p2t, tau = score
    t = xn.shape[0]
    n_exp = peer_u.shape[1]
    n_a = te // N_KEYS
    a_blk = max(n_a, 8)
    nh = N_RETR_HEADS
    once = pl.Buffered(1)
    return pl.pallas_call(
        functools.partial(_peer_dense_kernel, n_a=n_a, a_blk=a_blk),
        grid=(t // tm, n_exp // te),
        in_specs=[
            pl.BlockSpec((tm, D_MODEL), lambda i, e: (i, 0), pipeline_mode=once),
            pl.BlockSpec((None, te, D_MODEL), lambda i, e: (l, e, 0)),
            pl.BlockSpec((None, te, D_MODEL), lambda i, e: (l, e, 0)),
            pl.BlockSpec((nh, a_blk, tm), lambda i, e: (0, e * n_a // a_blk, i)),
            pl.BlockSpec((nh, a_blk, tm), lambda i, e: (0, e * n_a // a_blk, i)),
            pl.BlockSpec((nh, N_KEYS, tm), lambda i, e: (0, 0, i), pipeline_mode=once),
            pl.BlockSpec((nh, N_KEYS, tm), lambda i, e: (0, 0, i), pipeline_mode=once),
            pl.BlockSpec((nh, 8, tm), lambda i, e: (0, 0, i)),
            pl.BlockSpec((tm, D_MODEL), lambda i, e: (i, 0), pipeline_mode=once),
        ],
        out_specs=pl.BlockSpec((tm, D_MODEL), lambda i, e: (i, 0)),
        out_shape=jax.ShapeDtypeStruct((t, D_MODEL), F32),
        compiler_params=_cparams("parallel", "arbitrary"),
        name="peer_dense",
    )(xn, peer_u, peer_v, s1t, p1t, s2t, p2t, tau, x_res)


def kernel(x_prompt, x_sample, mem_prompt, cache_k, cache_v, cache_mem_k, cache_mem_v, state_ssm_re, state_ssm_im, state_pool, page_table, norm_mix, w_in, ssm_a_re, ssm_a_im, ssm_log_dt, ssm_b_re, ssm_b_im, ssm_c_re, ssm_c_im, ssm_d, ssm_w_glu, pool_w, pool_scale, att_lq1, att_lk1, att_lq2, att_lk2, att_subln, w_up_ssm, w_up_pool, w_up_att, w_out, norm_cross, norm_mem, w_mem_q, w_mem_k, w_mem_v, w_mem_o, norm_ffn, peer_w_query, peer_keys1, peer_keys2, peer_u, peer_v, norm_final):
    n_p, seq, d = x_prompt.shape
    n_s, dec_seq, _ = x_sample.shape
    depth = w_in.shape[0]
    tp = n_p * seq
    ts = n_s * dec_seq
    past_len = page_table.shape[1] * cache_k.shape[2]

    tm = 768
    tm_mm = 384
    tc_p, tc_s = 128, dec_seq

    x = jnp.concatenate([x_prompt.reshape(tp, d), x_sample.reshape(ts, d)], axis=0)
    mem_rows = mem_prompt.reshape(n_p * N_MEM, d)
    cmk = cache_mem_k.reshape(depth, n_s, N_MEM, d)
    cmv = cache_mem_v.reshape(depth, n_s, N_MEM, d)
    zeros_h = jnp.zeros((n_p, N_STATE), F32)
    zeros_buf = jnp.zeros((n_p, POOL_CARRY, W_POOL), F32)
    sample_buf = jnp.pad(state_pool, ((0, 0), (0, 0), (POOL_CARRY - POOL_BUF, 0), (0, 0)))
    (w_in, w_up_ssm, w_up_pool, w_up_att, w_out, w_mem_q, w_mem_k, w_mem_v, w_mem_o,
     peer_w_query, peer_u, peer_v) = (
        w.astype(BF16) for w in (w_in, w_up_ssm, w_up_pool, w_up_att, w_out, w_mem_q, w_mem_k,
                                 w_mem_v, w_mem_o, peer_w_query, peer_u, peer_v))

    outs = {k: [] for k in ("kp", "vp", "mkp", "mvp", "hrp", "hip", "bp", "ks", "vs", "hrs", "his", "bs")}
    for l in range(depth):
        a_mat = _in_proj(x, norm_mix[l], w_in, l, 0, COL_GATE, gate=False, tm=tm, tn=1024)
        gates = _in_proj(x, norm_mix[l], w_in, l, COL_GATE, 3 * D_MODEL, gate=True, tm=tm, tn=1024)
        k_all = a_mat[:, COL_K:COL_V]
        v_all = a_mat[:, COL_V:COL_GATE]
        outs["kp"].append(k_all[:tp].reshape(n_p, seq, N_HEADS_ATT, W_HEAD))
        outs["vp"].append(v_all[:tp].reshape(n_p, seq, N_HEADS_ATT, W_HEAD))
        outs["ks"].append(k_all[tp:].reshape(n_s, dec_seq, N_HEADS_ATT, W_HEAD))
        outs["vs"].append(v_all[tp:].reshape(n_s, dec_seq, N_HEADS_ATT, W_HEAD))

        s5_params = (ssm_a_re[l], ssm_a_im[l], ssm_log_dt[l], ssm_b_re[l], ssm_b_im[l],
                     ssm_c_re[l], ssm_c_im[l])
        z_p, hrp, hip = _s5(a_mat, 0, n_p, seq, _s5_tables(*s5_params, tc_p), ssm_d[l],
                            zeros_h, zeros_h, ssm_w_glu, l, tc=tc_p)
        z_s, hrs, his = _s5(a_mat, tp, n_s, dec_seq, _s5_tables(*s5_params, tc_s), ssm_d[l],
                            state_ssm_re[l].reshape(n_s, N_STATE), state_ssm_im[l].reshape(n_s, N_STATE),
                            ssm_w_glu, l, tc=tc_s)
        outs["hrp"].append(hrp); outs["hip"].append(hip)
        outs["hrs"].append(hrs); outs["his"].append(his)

        yb_p, bp = _pool(a_mat, 0, n_p, seq, zeros_buf, pool_w, pool_scale[l], l, 0, tc=tc_p)
        yb_s, bs = _pool(a_mat, tp, n_s, dec_seq, sample_buf[l], pool_w, pool_scale[l], l, past_len, tc=tc_s)
        outs["bp"].append(bp); outs["bs"].append(bs)

        lam_init = 0.8 - 0.6 * math.exp(-0.3 * l)
        lam = (jnp.exp(jnp.sum(att_lq1[l] * att_lk1[l])) - jnp.exp(jnp.sum(att_lq2[l] * att_lk2[l])) + lam_init)
        lam_row = jnp.full((8, W_HEAD), lam, F32)
        yc_p = _attn_prompt(a_mat, n_p, seq, lam_row, att_subln[l], lam_init, tq=512, hb=2)
        yc_s = _attn_sample(a_mat, tp, n_s, dec_seq, cache_k, cache_v, page_table, lam_row,
                            att_subln[l], lam_init, l, pages_per_step=8)

        z = jnp.concatenate([z_p, z_s], axis=0)
        yb = jnp.concatenate([yb_p, yb_s], axis=0)
        yc = jnp.concatenate([yc_p, yc_s], axis=0)
        merged = _merge(z, yb, yc, gates, w_up_ssm, w_up_pool, w_up_att, l, tm=tm, tn=512)
        x = _matmul(merged, w_out, l, res=x, tm=tm_mm, name="w_out")

        mk_p = _matmul(mem_rows, w_mem_k, l, gain=norm_mem[l], tm=N_MEM, name="mem_k")
        mv_p = _matmul(mem_rows, w_mem_v, l, gain=norm_mem[l], tm=N_MEM, name="mem_v")
        outs["mkp"].append(mk_p.reshape(n_p, N_MEM, N_HEADS_MEM, HEAD_DIM_MEM))
        outs["mvp"].append(mv_p.reshape(n_p, N_MEM, N_HEADS_MEM, HEAD_DIM_MEM))
        qm = _matmul(x, w_mem_q, l, gain=norm_cross[l], tm=tm_mm, name="mem_q")
        ca_p = _cross_attn(qm, 0, n_p, seq, mk_p.reshape(n_p, N_MEM, d), mv_p.reshape(n_p, N_MEM, d),
                           lambda n: (n, 0, 0), tq=512)
        ca_s = _cross_attn(qm, tp, n_s, dec_seq, cmk, cmv, lambda n: (l, n, 0, 0), tq=dec_seq)
        x = _matmul(jnp.concatenate([ca_p, ca_s], axis=0), w_mem_o, l, res=x, tm=tm_mm, name="mem_o")

        xn, *score = _peer_score(x, norm_ffn[l], peer_w_query, peer_keys1, peer_keys2, l, tm=tm)
        x = _peer_dense(xn, x, score, peer_u, peer_v, l, tm=tm, te=512)

    y = _final_norm(x, norm_final, tm=tm)
    st = lambda k: jnp.stack(outs[k])
    return (y[:tp].reshape(n_p, seq, d), y[tp:].reshape(n_s, dec_seq, d),
            st("kp"), st("vp"), st("mkp"), st("mvp"), st("hrp"), st("hip"), st("bp"),
            st("ks"), st("vs"), st("hrs"), st("his"), st("bs"))
```

```python
import functools
import math

import jax
import jax.numpy as jnp
from jax import lax
from jax.experimental import pallas as pl
from jax.experimental.pallas import tpu as pltpu

F32 = jnp.float32
BF16 = jnp.bfloat16
EPS = 1e-6
NEG_INF = -1e30
LOG2E = math.log2(math.e)

D_MODEL = 2048
W_SSM = 512
SSM_GROUP = 16
N_SSM_GROUPS = 32
SSM_STATE = 64
N_STATE = N_SSM_GROUPS * SSM_STATE
W_POOL = 512
POOL_WINDOWS = (2, 4, 8, 16)
POOL_GROUP = 128
POOL_BUF = 15
POOL_CARRY = 16
SCAN_GROUP = 8
N_HEADS_ATT = 8
HEAD_DIM_ATT = 64
W_HEAD = 2 * HEAD_DIM_ATT
W_ATT = N_HEADS_ATT * W_HEAD
N_MEM = 256
N_HEADS_MEM = 4
HEAD_DIM_MEM = 512
N_KEYS = 128
N_RETR_HEADS = 8
D_QUERY = 256
TOPK = 16

COL_SSM = 0
COL_POOL = W_SSM
COL_Q = W_SSM + W_POOL
COL_K = COL_Q + W_ATT
COL_V = COL_K + W_ATT
COL_GATE = COL_V + W_ATT
IN_COLS = COL_GATE + 3 * D_MODEL

VMEM_LIMIT_BYTES = 56 * 1024 * 1024
MXU_DEPTH = 256
NT_DIMS = (((1,), (1,)), ((), ()))


def _cparams(*sem):
    return pltpu.CompilerParams(dimension_semantics=sem, vmem_limit_bytes=VMEM_LIMIT_BYTES)


def _bdot(a, b):
    return jnp.dot(a.astype(BF16), b.astype(BF16), preferred_element_type=F32)


def _bdot_nt(a, b):
    return lax.dot_general(a.astype(BF16), b.astype(BF16), NT_DIMS, preferred_element_type=F32)


def _rms(x, g):
    return x * lax.rsqrt(jnp.mean(x * x, axis=-1, keepdims=True) + EPS) * g


def _gelu(x):
    return 0.5 * x * (1.0 + lax.erf(x * (2.0 ** -0.5)))


def _in_proj_kernel(x_ref, g_ref, w_ref, o_ref, xn_ref, *, gate):
    @pl.when(pl.program_id(1) == 0)
    def _():
        xn_ref[...] = _rms(x_ref[...], g_ref[...]).astype(BF16)

    res = jnp.dot(xn_ref[...], w_ref[...], preferred_element_type=F32)
    o_ref[...] = jax.nn.sigmoid(res) if gate else res


def _in_proj(x, g, w_in, l, col0, n_cols, *, gate, tm, tn):
    t = x.shape[0]
    jb0 = col0 // tn
    return pl.pallas_call(
        functools.partial(_in_proj_kernel, gate=gate),
        grid=(t // tm, n_cols // tn),
        in_specs=[
            pl.BlockSpec((tm, D_MODEL), lambda i, j: (i, 0)),
            pl.BlockSpec((1, D_MODEL), lambda i, j: (0, 0)),
            pl.BlockSpec((None, D_MODEL, tn), lambda i, j: (l, 0, jb0 + j)),
        ],
        out_specs=pl.BlockSpec((tm, tn), lambda i, j: (i, j)),
        out_shape=jax.ShapeDtypeStruct((t, n_cols), F32),
        scratch_shapes=[pltpu.VMEM((tm, D_MODEL), BF16)],
        compiler_params=_cparams("parallel", "arbitrary"),
        name="in_gate" if gate else "in_proj",
    )(x, g.reshape(1, D_MODEL), w_in)


def _mm_kernel(*refs, has_norm, has_res):
    x_ref = refs[0]
    pos = 1
    g_ref = None
    if has_norm:
        g_ref = refs[pos]
        pos += 1
    w_ref = refs[pos]
    pos += 1
    r_ref = None
    if has_res:
        r_ref = refs[pos]
        pos += 1
    o_ref = refs[pos]
    x = x_ref[...]
    if has_norm:
        x = _rms(x, g_ref[...])
    res = jnp.dot(x.astype(BF16), w_ref[...], preferred_element_type=F32)
    if has_res:
        res = res + r_ref[...]
    o_ref[...] = res


def _matmul(x, w, l, *, gain=None, res=None, tm, name):
    t, k = x.shape
    n = w.shape[2]
    has_norm = gain is not None
    has_res = res is not None
    in_specs = [pl.BlockSpec((tm, k), lambda i: (i, 0))]
    args = [x]
    if has_norm:
        in_specs.append(pl.BlockSpec((1, k), lambda i: (0, 0)))
        args.append(gain.reshape(1, k))
    in_specs.append(pl.BlockSpec((None, k, n), lambda i: (l, 0, 0), pipeline_mode=pl.Buffered(1)))
    args.append(w)
    if has_res:
        in_specs.append(pl.BlockSpec((tm, n), lambda i: (i, 0)))
        args.append(res)
    return pl.pallas_call(
        functools.partial(_mm_kernel, has_norm=has_norm, has_res=has_res),
        grid=(t // tm,),
        in_specs=in_specs,
        out_specs=pl.BlockSpec((tm, n), lambda i: (i, 0)),
        out_shape=jax.ShapeDtypeStruct((t, n), F32),
        compiler_params=_cparams("parallel"),
        name=name,
    )(*args)


def _merge_kernel(z_ref, yb_ref, yc_ref, g0_ref, g1_ref, g2_ref, wa_ref, wb_ref, wc_ref, o_ref):
    o_ref[...] = (g0_ref[...] * _bdot(z_ref[...], wa_ref[...])
                  + g1_ref[...] * _bdot(yb_ref[...], wb_ref[...])
                  + g2_ref[...] * _bdot(yc_ref[...], wc_ref[...]))


def _merge(z, yb, yc, gates, w_up_ssm, w_up_pool, w_up_att, l, *, tm, tn):
    t = z.shape[0]
    nb = D_MODEL // tn
    return pl.pallas_call(
        _merge_kernel,
        grid=(t // tm, nb),
        in_specs=[
            pl.BlockSpec((tm, W_SSM), lambda i, j: (i, 0)),
            pl.BlockSpec((tm, W_POOL), lambda i, j: (i, 0)),
            pl.BlockSpec((tm, W_ATT), lambda i, j: (i, 0)),
            pl.BlockSpec((tm, tn), lambda i, j: (i, j)),
            pl.BlockSpec((tm, tn), lambda i, j: (i, nb + j)),
            pl.BlockSpec((tm, tn), lambda i, j: (i, 2 * nb + j)),
            pl.BlockSpec((None, W_SSM, tn), lambda i, j: (l, 0, j)),
            pl.BlockSpec((None, W_POOL, tn), lambda i, j: (l, 0, j)),
            pl.BlockSpec((None, W_ATT, tn), lambda i, j: (l, 0, j)),
        ],
        out_specs=pl.BlockSpec((tm, tn), lambda i, j: (i, j)),
        out_shape=jax.ShapeDtypeStruct((t, D_MODEL), F32),
        compiler_params=_cparams("parallel", "arbitrary"),
        name="merge_up",
    )(z, yb, yc, gates, gates, gates, w_up_ssm, w_up_pool, w_up_att)


def _final_norm_kernel(x_ref, g_ref, o_ref):
    o_ref[...] = _rms(x_ref[...], g_ref[...])


def _final_norm(x, g, *, tm):
    t = x.shape[0]
    return pl.pallas_call(
        _final_norm_kernel,
        grid=(t // tm,),
        in_specs=[pl.BlockSpec((tm, D_MODEL), lambda i: (i, 0)),
                  pl.BlockSpec((1, D_MODEL), lambda i: (0, 0))],
        out_specs=pl.BlockSpec((tm, D_MODEL), lambda i: (i, 0)),
        out_shape=jax.ShapeDtypeStruct((t, D_MODEL), F32),
        compiler_params=_cparams("parallel"),
        name="final_norm",
    )(x, g.reshape(1, D_MODEL))


def _s5_tables(a_re, a_im, log_dt, b_re, b_im, c_re, c_im):
    g_, p_ = N_SSM_GROUPS, SSM_STATE
    dt = jnp.exp(log_dt)[:, None]
    mag = jnp.exp(a_re * dt)
    lb_re = mag * jnp.cos(a_im * dt)
    lb_im = mag * jnp.sin(a_im * dt)
    den = a_re * a_re + a_im * a_im
    nr = lb_re - 1.0
    f_re = (nr * a_re + lb_im * a_im) / den
    f_im = (lb_im * a_re - nr * a_im) / den
    bb_re = f_re[..., None] * b_re - f_im[..., None] * b_im
    bb_im = f_re[..., None] * b_im + f_im[..., None] * b_re
    eye = jnp.eye(g_, dtype=F32)
    bd_re = jnp.einsum('gpc,gh->gchp', bb_re, eye).reshape(W_SSM, N_STATE)
    bd_im = jnp.einsum('gpc,gh->gchp', bb_im, eye).reshape(W_SSM, N_STATE)
    bmat = jnp.concatenate([bd_re, bd_im], axis=1).astype(BF16)
    cd_re = jnp.einsum('gcp,gh->gphc', c_re, eye).reshape(N_STATE, W_SSM).astype(BF16)
    cd_im = jnp.einsum('gcp,gh->gphc', -c_im, eye).reshape(N_STATE, W_SSM).astype(BF16)
    lr = lb_re.reshape(1, N_STATE)
    li = lb_im.reshape(1, N_STATE)
    nsteps = int(math.log2(SCAN_GROUP))
    sq_re, sq_im = [lr], [li]
    for _ in range(nsteps):
        r, i = sq_re[-1], sq_im[-1]
        sq_re.append(r * r - i * i)
        sq_im.append(2.0 * r * i)
    pt_re, pt_im = lr, li
    for s in range(nsteps):
        r, i = sq_re[s], sq_im[s]
        pt_re, pt_im = (jnp.concatenate([pt_re, pt_re * r - pt_im * i], axis=0),
                        jnp.concatenate([pt_im, pt_re * i + pt_im * r], axis=0))
    pw_re = jnp.concatenate(sq_re[:nsteps], axis=0)
    pw_im = jnp.concatenate(sq_im[:nsteps], axis=0)
    return bmat, cd_re, cd_im, pw_re, pw_im, pt_re, pt_im


def _s5_kernel(u_ref, b_ref, cre_ref, cim_ref, d_ref, pwr_ref, pwi_ref, ptr_ref, pti_ref,
               h0r_ref, h0i_ref, wglu_ref, z_ref, hr_ref, hi_ref, cr_ref, ci_ref, *, tc, nsteps):
    c = pl.program_id(1)

    @pl.when(c == 0)
    def _():
        cr_ref[...] = h0r_ref[...]
        ci_ref[...] = h0i_ref[...]

    u = u_ref[...]
    bu = jnp.dot(u.astype(BF16), b_ref[...], preferred_element_type=F32)
    re = bu[:, :N_STATE]
    im = bu[:, N_STATE:]
    row = lax.broadcasted_iota(jnp.int32, (tc, 1), 0) % SCAN_GROUP
    for s in range(nsteps):
        k = 1 << s
        keep = row >= k
        sre = jnp.where(keep, pltpu.roll(re, k, 0), 0.0)
        sim = jnp.where(keep, pltpu.roll(im, k, 0), 0.0)
        ar = pwr_ref[s:s + 1, :]
        ai = pwi_ref[s:s + 1, :]
        re, im = re + (ar * sre - ai * sim), im + (ar * sim + ai * sre)
    hr = cr_ref[...]
    hi = ci_ref[...]
    pr = ptr_ref[...]
    pi = pti_ref[...]
    res, ims = [], []
    for j in range(0, tc, SCAN_GROUP):
        gr = re[j:j + SCAN_GROUP] + (pr * hr - pi * hi)
        gi = im[j:j + SCAN_GROUP] + (pr * hi + pi * hr)
        hr = gr[SCAN_GROUP - 1:SCAN_GROUP, :]
        hi = gi[SCAN_GROUP - 1:SCAN_GROUP, :]
        res.append(gr)
        ims.append(gi)
    re = jnp.concatenate(res, axis=0)
    im = jnp.concatenate(ims, axis=0)
    cr_ref[...] = hr
    ci_ref[...] = hi
    y = (jnp.dot(re.astype(BF16), cre_ref[...], preferred_element_type=F32)
         + jnp.dot(im.astype(BF16), cim_ref[...], preferred_element_type=F32)
         + d_ref[...] * u)
    z = _gelu(y)
    z_ref[...] = z * jax.nn.sigmoid(_bdot(z, wglu_ref[...]))

    @pl.when(c == pl.num_programs(1) - 1)
    def _():
        hr_ref[...] = re[tc - 1:tc, :]
        hi_ref[...] = im[tc - 1:tc, :]


def _s5(a_mat, row0, n_seq, seq_len, tables, d_skip, h0_re, h0_im, w_glu, l, *, tc):
    bmat, cd_re, cd_im, pw_re, pw_im, pt_re, pt_im = tables
    nsteps = pw_re.shape[0]
    assert tc % SCAN_GROUP == 0
    nc = seq_len // tc
    rb0 = row0 // tc
    const = lambda shape: pl.BlockSpec(shape, lambda n, c: tuple(0 for _ in shape))
    z, hr, hi = pl.pallas_call(
        functools.partial(_s5_kernel, tc=tc, nsteps=nsteps),
        grid=(n_seq, nc),
        in_specs=[
            pl.BlockSpec((tc, W_SSM), lambda n, c: (rb0 + n * nc + c, COL_SSM // W_SSM)),
            const((W_SSM, 2 * N_STATE)),
            const((N_STATE, W_SSM)),
            const((N_STATE, W_SSM)),
            const((1, W_SSM)),
            const(pw_re.shape),
            const(pw_im.shape),
            const((SCAN_GROUP, N_STATE)),
            const((SCAN_GROUP, N_STATE)),
            pl.BlockSpec((None, 1, N_STATE), lambda n, c: (n, 0, 0)),
            pl.BlockSpec((None, 1, N_STATE), lambda n, c: (n, 0, 0)),
            pl.BlockSpec((None, W_SSM, W_SSM), lambda n, c: (l, 0, 0)),
        ],
        out_specs=[
            pl.BlockSpec((tc, W_SSM), lambda n, c: (n * nc + c, 0)),
            pl.BlockSpec((None, 1, N_STATE), lambda n, c: (n, 0, 0)),
            pl.BlockSpec((None, 1, N_STATE), lambda n, c: (n, 0, 0)),
        ],
        out_shape=[jax.ShapeDtypeStruct((n_seq * seq_len, W_SSM), F32),
                   jax.ShapeDtypeStruct((n_seq, 1, N_STATE), F32),
                   jax.ShapeDtypeStruct((n_seq, 1, N_STATE), F32)],
        scratch_shapes=[pltpu.VMEM((1, N_STATE), F32), pltpu.VMEM((1, N_STATE), F32)],
        compiler_params=_cparams("parallel", "arbitrary"),
        name="s5",
    )(a_mat, bmat, cd_re, cd_im, d_skip.reshape(1, W_SSM), pw_re, pw_im, pt_re, pt_im,
      h0_re.reshape(n_seq, 1, N_STATE), h0_im.reshape(n_seq, 1, N_STATE), w_glu)
    shp = (n_seq, N_SSM_GROUPS, SSM_STATE)
    return z, hr.reshape(shp), hi.reshape(shp)


def _pool_kernel(u_ref, buf_ref, w_ref, sc_ref, y_ref, nb_ref, xx_ref, *, tc, start_pos):
    c = pl.program_id(1)

    @pl.when(c == 0)
    def _():
        xx_ref[0:POOL_CARRY, :] = buf_ref[...]

    u = u_ref[...]
    xx_ref[POOL_CARRY:, :] = u
    x = xx_ref[...]
    sums = []
    s = x
    for w in (1, 2, 4, 8):
        s = s + pltpu.roll(s, w, 0)
        sums.append(s)
    t_abs = (start_pos + c * tc + lax.broadcasted_iota(jnp.int32, (tc, 1), 0) + 1).astype(F32)
    outs = []
    for gi, w in enumerate(POOL_WINDOWS):
        sl = slice(gi * POOL_GROUP, (gi + 1) * POOL_GROUP)
        mean = sums[gi][POOL_CARRY:, sl] / jnp.minimum(t_abs, float(w))
        d = mean - u[:, sl]
        outs.append(_bdot(d, w_ref[gi]))
    y_ref[...] = jnp.concatenate(outs, axis=-1) * sc_ref[...]
    carry = x[tc:, :]
    xx_ref[0:POOL_CARRY, :] = carry

    @pl.when(c == pl.num_programs(1) - 1)
    def _():
        nb_ref[...] = carry


def _pool(a_mat, row0, n_seq, seq_len, buf16, pool_w, pool_scale, l, start_pos, *, tc):
    nc = seq_len // tc
    rb0 = row0 // tc
    y, nb = pl.pallas_call(
        functools.partial(_pool_kernel, tc=tc, start_pos=start_pos),
        grid=(n_seq, nc),
        in_specs=[
            pl.BlockSpec((tc, W_POOL), lambda n, c: (rb0 + n * nc + c, COL_POOL // W_POOL)),
            pl.BlockSpec((None, POOL_CARRY, W_POOL), lambda n, c: (n, 0, 0)),
            pl.BlockSpec((None, len(POOL_WINDOWS), POOL_GROUP, POOL_GROUP), lambda n, c: (l, 0, 0, 0)),
            pl.BlockSpec((1, W_POOL), lambda n, c: (0, 0)),
        ],
        out_specs=[
            pl.BlockSpec((tc, W_POOL), lambda n, c: (n * nc + c, 0)),
            pl.BlockSpec((None, POOL_CARRY, W_POOL), lambda n, c: (n, 0, 0)),
        ],
        out_shape=[jax.ShapeDtypeStruct((n_seq * seq_len, W_POOL), F32),
                   jax.ShapeDtypeStruct((n_seq, POOL_CARRY, W_POOL), F32)],
        scratch_shapes=[pltpu.VMEM((POOL_CARRY + tc, W_POOL), F32)],
        compiler_params=_cparams("parallel", "arbitrary"),
        name="pool",
    )(a_mat, buf16, pool_w, pool_scale.reshape(1, W_POOL))
    return y, nb[:, POOL_CARRY - POOL_BUF:, :]


def _subln(o, g, lam_init):
    return _rms(o, g) * (1.0 - lam_init)


def _attn_prompt_kernel(qt_ref, kt_ref, q_ref, k_ref, v_ref, lam_ref, g_ref, o_ref,
                        m_ref, l_ref, acc_ref, *, tq, hb, lam_init):
    p = pl.program_id(2)
    qi = qt_ref[p]
    ki = kt_ref[p]

    @pl.when(ki == 0)
    def _():
        m_ref[...] = jnp.full(m_ref.shape, NEG_INF, F32)
        l_ref[...] = jnp.zeros(l_ref.shape, F32)
        acc_ref[...] = jnp.zeros(acc_ref.shape, F32)

    def accumulate(masked):
        q = (q_ref[...] * (HEAD_DIM_ATT ** -0.5 * LOG2E)).astype(BF16)
        k = k_ref[...].astype(BF16)
        v = v_ref[...].astype(BF16)
        if masked:
            causal = (lax.broadcasted_iota(jnp.int32, (tq, tq), 0)
                      >= lax.broadcasted_iota(jnp.int32, (tq, tq), 1))
        for hh in range(hb):
            vh = v[:, hh * W_HEAD:(hh + 1) * W_HEAD]
            for c in range(2):
                idx = 2 * hh + c
                sl = slice(hh * W_HEAD + c * HEAD_DIM_ATT, hh * W_HEAD + (c + 1) * HEAD_DIM_ATT)
                s = lax.dot_general(q[:, sl], k[:, sl], NT_DIMS, preferred_element_type=F32)
                if masked:
                    s = jnp.where(causal, s, NEG_INF)
                m_prev = m_ref[idx]
                m_new = jnp.maximum(m_prev, jnp.max(s, axis=-1, keepdims=True))
                alpha = jnp.exp2(m_prev - m_new)
                pexp = jnp.exp2(s - jnp.concatenate([m_new] * (tq // W_HEAD), axis=1))
                l_ref[idx] = alpha * l_ref[idx] + jnp.sum(pexp, axis=-1, keepdims=True)
                acc_ref[idx] = alpha * acc_ref[idx] + jnp.dot(pexp.astype(BF16), vh,
                                                              preferred_element_type=F32)
                m_ref[idx] = m_new

    @pl.when(ki < qi)
    def _():
        accumulate(False)

    @pl.when(ki == qi)
    def _():
        accumulate(True)
        lam = lam_ref[0:1, :]
        for hh in range(hb):
            o = (acc_ref[2 * hh] / l_ref[2 * hh]
                 - lam * (acc_ref[2 * hh + 1] / l_ref[2 * hh + 1]))
            o_ref[:, hh * W_HEAD:(hh + 1) * W_HEAD] = _subln(o, g_ref[...], lam_init)


def _attn_prompt(a_mat, n_seq, seq_len, lam_row, subln_g, lam_init, *, tq, hb):
    nq = seq_len // tq
    pairs = [(qi, ki) for qi in range(nq) for ki in range(qi + 1)]
    qt = jnp.asarray([p[0] for p in pairs], jnp.int32)
    kt = jnp.asarray([p[1] for p in pairs], jnp.int32)
    wb = hb * W_HEAD
    cq, ck, cv = COL_Q // wb, COL_K // wb, COL_V // wb
    grid_spec = pltpu.PrefetchScalarGridSpec(
        num_scalar_prefetch=2,
        grid=(n_seq, N_HEADS_ATT // hb, len(pairs)),
        in_specs=[
            pl.BlockSpec((tq, wb), lambda b, h, p, qt, kt: (b * nq + qt[p], cq + h)),
            pl.BlockSpec((tq, wb), lambda b, h, p, qt, kt: (b * nq + kt[p], ck + h)),
            pl.BlockSpec((tq, wb), lambda b, h, p, qt, kt: (b * nq + kt[p], cv + h)),
            pl.BlockSpec((8, W_HEAD), lambda b, h, p, qt, kt: (0, 0)),
            pl.BlockSpec((1, W_HEAD), lambda b, h, p, qt, kt: (0, 0)),
        ],
        out_specs=pl.BlockSpec((tq, wb), lambda b, h, p, qt, kt: (b * nq + qt[p], h)),
        scratch_shapes=[pltpu.VMEM((2 * hb, tq, W_HEAD), F32), pltpu.VMEM((2 * hb, tq, W_HEAD), F32),
                        pltpu.VMEM((2 * hb, tq, W_HEAD), F32)],
    )
    return pl.pallas_call(
        functools.partial(_attn_prompt_kernel, tq=tq, hb=hb, lam_init=lam_init),
        grid_spec=grid_spec,
        out_shape=jax.ShapeDtypeStruct((n_seq * seq_len, W_ATT), F32),
        compiler_params=_cparams("parallel", "parallel", "arbitrary"),
        name="attn_prompt",
    )(qt, kt, a_mat, a_mat, a_mat, lam_row, subln_g.reshape(1, W_HEAD))


def _attn_sample_kernel(pt_ref, q_ref, kn_ref, vn_ref, lam_ref, g_ref, *rest, n_pages, dec_seq, lam_init):
    k_refs = rest[:n_pages]
    v_refs = rest[n_pages:2 * n_pages]
    o_ref, qh_ref, bias_ref, m_ref, l_ref, acc_ref = rest[2 * n_pages:]
    step = pl.program_id(1)
    rph = 2 * dec_seq

    @pl.when(step == 0)
    def _():
        q = q_ref[...] * (HEAD_DIM_ATT ** -0.5)
        row = lax.broadcasted_iota(jnp.int32, (rph, W_HEAD), 0)
        lane = lax.broadcasted_iota(jnp.int32, (rph, W_HEAD), 1)
        keep = lane // HEAD_DIM_ATT == row // dec_seq
        for h in range(N_HEADS_ATT):
            qh = q[:, h * W_HEAD:(h + 1) * W_HEAD]
            qh_ref[h] = jnp.where(keep, jnp.concatenate([qh, qh], axis=0), 0.0).astype(BF16)
        m_ref[...] = jnp.full(m_ref.shape, NEG_INF, F32)
        l_ref[...] = jnp.zeros(l_ref.shape, F32)
        acc_ref[...] = jnp.zeros(acc_ref.shape, F32)
        brow = lax.broadcasted_iota(jnp.int32, bias_ref.shape, 0) // rph
        bcol = lax.broadcasted_iota(jnp.int32, bias_ref.shape, 1) % N_HEADS_ATT
        bias_ref[...] = jnp.where(brow == bcol, 0.0, NEG_INF)

    def update(s, pv_of):
        m_prev = m_ref[...]
        m_new = jnp.maximum(m_prev, jnp.max(s, axis=-1, keepdims=True))
        alpha = jnp.exp(m_prev - m_new)
        pexp = jnp.exp(s - m_new)
        l_ref[...] = alpha * l_ref[...] + jnp.sum(pexp, axis=-1, keepdims=True)
        acc_ref[...] = alpha * acc_ref[...] + pv_of(pexp.astype(BF16))
        m_ref[...] = m_new

    q_all = jnp.concatenate([qh_ref[h] for h in range(N_HEADS_ATT)], axis=0)
    k_all = jnp.concatenate([r[...].astype(BF16) for r in k_refs], axis=0)
    v_all = jnp.concatenate([r[...].astype(BF16) for r in v_refs], axis=0)
    s = lax.dot_general(q_all, k_all, NT_DIMS, preferred_element_type=F32)
    s = s + jnp.concatenate([bias_ref[...]] * n_pages, axis=1)
    update(s, lambda pb: jnp.dot(pb, v_all, preferred_element_type=F32))

    @pl.when(step == pl.num_programs(1) - 1)
    def _():
        pad = jnp.zeros((W_HEAD - dec_seq, W_HEAD), F32)

        def new_rows(ref, h):
            return jnp.concatenate([ref[:, h * W_HEAD:(h + 1) * W_HEAD], pad], axis=0).astype(BF16)

        s2 = jnp.concatenate(
            [lax.dot_general(qh_ref[h], new_rows(kn_ref, h), NT_DIMS, preferred_element_type=F32)
             for h in range(N_HEADS_ATT)], axis=0)
        qi = lax.broadcasted_iota(jnp.int32, s2.shape, 0) % dec_seq
        kj = lax.broadcasted_iota(jnp.int32, s2.shape, 1)
        s2 = jnp.where(qi >= kj, s2, NEG_INF)
        update(s2, lambda pb: jnp.concatenate(
            [jnp.dot(pb[h * rph:(h + 1) * rph], new_rows(vn_ref, h), preferred_element_type=F32)
             for h in range(N_HEADS_ATT)], axis=0))
        o = acc_ref[...] / l_ref[...]
        lam = lam_ref[0:1, :]
        for h in range(N_HEADS_ATT):
            r0 = h * rph
            oh = o[r0:r0 + dec_seq] - lam * o[r0 + dec_seq:r0 + rph]
            o_ref[:, h * W_HEAD:(h + 1) * W_HEAD] = _subln(oh, g_ref[...], lam_init)


def _attn_sample(a_mat, row0, n_seq, dec_seq, cache_k, cache_v, page_table, lam_row, subln_g,
                 lam_init, l, *, pages_per_step):
    depth, n_pool, page, _, _ = cache_k.shape
    ck = cache_k.reshape(depth, n_pool, page * N_HEADS_ATT, W_HEAD)
    cv = cache_v.reshape(depth, n_pool, page * N_HEADS_ATT, W_HEAD)
    n_pt = page_table.shape[1]
    r = pages_per_step
    rb0 = row0 // dec_seq
    cq, ckk, cvv = COL_Q // W_ATT, COL_K // W_ATT, COL_V // W_ATT

    def page_spec(j):
        return pl.BlockSpec((None, None, page * N_HEADS_ATT, W_HEAD),
                            lambda n, s, pt: (l, pt[n, s * r + j], 0, 0))

    grid_spec = pltpu.PrefetchScalarGridSpec(
        num_scalar_prefetch=1,
        grid=(n_seq, n_pt // r),
        in_specs=[
            pl.BlockSpec((dec_seq, W_ATT), lambda n, s, pt: (rb0 + n, cq)),
            pl.BlockSpec((dec_seq, W_ATT), lambda n, s, pt: (rb0 + n, ckk)),
            pl.BlockSpec((dec_seq, W_ATT), lambda n, s, pt: (rb0 + n, cvv)),
            pl.BlockSpec((8, W_HEAD), lambda n, s, pt: (0, 0)),
            pl.BlockSpec((1, W_HEAD), lambda n, s, pt: (0, 0)),
        ] + [page_spec(j) for j in range(r)] + [page_spec(j) for j in range(r)],
        out_specs=pl.BlockSpec((dec_seq, W_ATT), lambda n, s, pt: (n, 0)),
        scratch_shapes=[pltpu.VMEM((N_HEADS_ATT, 2 * dec_seq, W_HEAD), BF16),
                        pltpu.VMEM((2 * N_HEADS_ATT * dec_seq, page * N_HEADS_ATT), F32),
                        pltpu.VMEM((2 * N_HEADS_ATT * dec_seq, 1), F32),
                        pltpu.VMEM((2 * N_HEADS_ATT * dec_seq, 1), F32),
                        pltpu.VMEM((2 * N_HEADS_ATT * dec_seq, W_HEAD), F32)],
    )
    return pl.pallas_call(
        functools.partial(_attn_sample_kernel, n_pages=r, dec_seq=dec_seq, lam_init=lam_init),
        grid_spec=grid_spec,
        out_shape=jax.ShapeDtypeStruct((n_seq * dec_seq, W_ATT), F32),
        compiler_params=_cparams("parallel", "arbitrary"),
        name="attn_sample",
    )(page_table, a_mat, a_mat, a_mat, lam_row, subln_g.reshape(1, W_HEAD),
      *([ck] * r), *([cv] * r))


def _cross_kernel(q_ref, k_ref, v_ref, o_ref):
    q = q_ref[...]
    scale = HEAD_DIM_MEM ** -0.5
    for h in range(N_HEADS_MEM):
        sl = slice(h * HEAD_DIM_MEM, (h + 1) * HEAD_DIM_MEM)
        s = _bdot_nt(q[:, sl], k_ref[:, sl]) * scale
        s = s - jnp.max(s, axis=-1, keepdims=True)
        e = jnp.exp(s)
        pr = e / jnp.sum(e, axis=-1, keepdims=True)
        o_ref[:, sl] = _bdot(pr, v_ref[:, sl])


def _cross_attn(qm, row0, n_seq, seq_len, mem_k, mem_v, kv_index, *, tq):
    nq = seq_len // tq
    rb0 = row0 // tq
    blk = tuple(None for _ in range(mem_k.ndim - 2)) + (N_MEM, D_MODEL)
    return pl.pallas_call(
        _cross_kernel,
        grid=(n_seq, nq),
        in_specs=[
            pl.BlockSpec((tq, D_MODEL), lambda n, i: (rb0 + n * nq + i, 0)),
            pl.BlockSpec(blk, lambda n, i: kv_index(n)),
            pl.BlockSpec(blk, lambda n, i: kv_index(n)),
        ],
        out_specs=pl.BlockSpec((tq, D_MODEL), lambda n, i: (n * nq + i, 0)),
        out_shape=jax.ShapeDtypeStruct((n_seq * seq_len, D_MODEL), F32),
        compiler_params=_cparams("parallel", "arbitrary"),
        name="cross_attn",
    )(qm, mem_k, mem_v)


def _oddeven_sort_pairs(n):
    pairs = []

    def merge(lo, hi, r):
        step = r * 2
        if step < hi - lo:
            merge(lo, hi, step)
            merge(lo + r, hi, step)
            pairs.extend((i, i + r) for i in range(lo + r, hi - r, step))
        else:
            pairs.append((lo, lo + r))

    def sort(lo, hi):
        if hi - lo >= 1:
            mid = lo + (hi - lo) // 2
            sort(lo, mid)
            sort(mid + 1, hi)
            merge(lo, hi, 1)

    sort(0, n - 1)
    return pairs


_SORT16 = _oddeven_sort_pairs(TOPK)


def _cmpx(xs, i, j):
    a, b = xs[i], xs[j]
    if a is None:
        xs[i], xs[j] = b, None
    elif b is not None:
        xs[i], xs[j] = jnp.maximum(a, b), jnp.minimum(a, b)


def _sort16_desc(xs):
    xs = list(xs)
    for i, j in _SORT16:
        _cmpx(xs, i, j)
    return xs


def _merge_top16(a, b):
    c = []
    for i in range(TOPK):
        x, y = a[i], b[TOPK - 1 - i]
        c.append(y if x is None else (x if y is None else jnp.maximum(x, y)))
    stride = TOPK // 2
    while stride:
        for i in range(TOPK):
            if not i & stride:
                _cmpx(c, i, i + stride)
        stride //= 2
    return c


def _top16_rows(st):
    lists = _sort16_desc([st[8 * v:8 * v + 8, :] for v in range(N_KEYS // 8)])
    for shift in (4, 2, 1):
        lists = _merge_top16(lists, [pltpu.roll(x, shift, 0) for x in lists])
    return lists


_CAND_PAIRS = [(i, j) for i in range(TOPK) for j in range(TOPK) if (i + 1) * (j + 1) <= TOPK]


def _peer_score_kernel(x_ref, g_ref, wq_ref, k1_ref, k2_ref,
                       xn_ref, s1_ref, p1_ref, s2_ref, p2_ref, tau_ref):
    h = pl.program_id(1)

    @pl.when(h == 0)
    def _():
        xn_ref[...] = _rms(x_ref[...], g_ref[...]).astype(BF16)

    q = jnp.dot(xn_ref[...], wq_ref[...].astype(BF16), preferred_element_type=F32)
    half = D_QUERY // 2
    s1 = _bdot_nt(k1_ref[...], q[:, :half])
    s2 = _bdot_nt(k2_ref[...], q[:, half:])
    v1 = _top16_rows(s1)
    v2 = _top16_rows(s2)
    cands = [v1[i] + v2[j] for i, j in _CAND_PAIRS]
    cands += [None] * (-len(cands) % TOPK)
    groups = [_sort16_desc(cands[g:g + TOPK]) for g in range(0, len(cands), TOPK)]
    top = groups[0]
    for grp in groups[1:]:
        top = _merge_top16(top, grp)
    z = jnp.exp(top[0] - top[0])
    for t in top[1:]:
        z = z + jnp.exp(t - top[0])
    s1_ref[...] = s1
    s2_ref[...] = s2
    p1_ref[...] = jnp.exp(s1 - v1[0][0:1, :]) / z[0:1, :]
    p2_ref[...] = jnp.exp(s2 - v2[0][0:1, :])
    tau_ref[...] = top[TOPK - 1]


def _peer_score(x, g, w_query, keys1, keys2, l, *, tm):
    t = x.shape[0]
    nh = N_RETR_HEADS
    key_spec = pl.BlockSpec((None, None, N_KEYS, D_QUERY // 2), lambda i, h: (l, h, 0, 0))
    st_spec = pl.BlockSpec((None, N_KEYS, tm), lambda i, h: (h, 0, i))
    st_shape = jax.ShapeDtypeStruct((nh, N_KEYS, t), F32)
    return pl.pallas_call(
        _peer_score_kernel,
        grid=(t // tm, nh),
        in_specs=[
            pl.BlockSpec((tm, D_MODEL), lambda i, h: (i, 0)),
            pl.BlockSpec((1, D_MODEL), lambda i, h: (0, 0)),
            pl.BlockSpec((None, D_MODEL, D_QUERY), lambda i, h: (l, 0, h)),
            key_spec, key_spec,
        ],
        out_specs=[
            pl.BlockSpec((tm, D_MODEL), lambda i, h: (i, 0)),
            st_spec, st_spec, st_spec, st_spec,
            pl.BlockSpec((None, 8, tm), lambda i, h: (h, 0, i)),
        ],
        out_shape=[jax.ShapeDtypeStruct((t, D_MODEL), BF16),
                   st_shape, st_shape, st_shape, st_shape,
                   jax.ShapeDtypeStruct((nh, 8, t), F32)],
        compiler_params=_cparams("parallel", "arbitrary"),
        name="peer_score",
    )(x, g.reshape(1, D_MODEL), w_query, keys1, keys2)


def _peer_dense_kernel(xn_ref, u_ref, v_ref, s1_ref, p1_ref, s2_ref, p2_ref, tau_ref, x_ref,
                       o_ref, *, n_a, a_blk):
    e = pl.program_id(1)

    @pl.when(e == 0)
    def _():
        o_ref[...] = x_ref[...]

    a0 = (e % (a_blk // n_a)) * n_a
    s1_rows = [[s1_ref[h, pl.ds(a0 + al, 1), :] for h in range(N_RETR_HEADS)] for al in range(n_a)]
    p1_rows = [[p1_ref[h, pl.ds(a0 + al, 1), :] for h in range(N_RETR_HEADS)] for al in range(n_a)]
    xn = xn_ref[...]
    a_per = MXU_DEPTH // N_KEYS
    groups = range(0, n_a, a_per)
    hts = [lax.dot_general(u_ref[g0 * N_KEYS:g0 * N_KEYS + MXU_DEPTH, :], xn, NT_DIMS,
                           preferred_element_type=F32) for g0 in groups]
    for ht, g0 in zip(hts, groups):
        blocks = []
        for al in range(g0, g0 + a_per):
            row = []
            for c in range(0, xn_ref.shape[0], N_KEYS):
                tok = slice(c, c + N_KEYS)
                g = None
                for h in range(N_RETR_HEADS):
                    s = s1_rows[al][h][:, tok] + s2_ref[h, :, tok]
                    w = p1_rows[al][h][:, tok] * p2_ref[h, :, tok]
                    hit = jnp.where(s >= tau_ref[h, 0:1, tok], w, 0.0)
                    g = hit if g is None else g + hit
                r0 = (al - g0) * N_KEYS
                row.append(g * _gelu(ht[r0:r0 + N_KEYS, tok]))
            blocks.append(jnp.concatenate(row, axis=1))
        wt = jnp.concatenate(blocks, axis=0)
        o_ref[...] += jnp.dot(wt.T.astype(BF16), v_ref[g0 * N_KEYS:g0 * N_KEYS + MXU_DEPTH, :],
                              preferred_element_type=F32)


def _peer_dense(xn, x_res, score, peer_u, peer_v, l, *, tm, te):
    s1t, p1t, s2t, p2t, tau = score
    t = xn.shape[0]
    n_exp = peer_u.shape[1]
    n_a = te // N_KEYS
    a_blk = max(n_a, 8)
    nh = N_RETR_HEADS
    assert te % MXU_DEPTH == 0
    once = pl.Buffered(1)
    return pl.pallas_call(
        functools.partial(_peer_dense_kernel, n_a=n_a, a_blk=a_blk),
        grid=(t // tm, n_exp // te),
        in_specs=[
            pl.BlockSpec((tm, D_MODEL), lambda i, e: (i, 0), pipeline_mode=once),
            pl.BlockSpec((None, te, D_MODEL), lambda i, e: (l, e, 0)),
            pl.BlockSpec((None, te, D_MODEL), lambda i, e: (l, e, 0)),
            pl.BlockSpec((nh, a_blk, tm), lambda i, e: (0, e * n_a // a_blk, i)),
            pl.BlockSpec((nh, a_blk, tm), lambda i, e: (0, e * n_a // a_blk, i)),
            pl.BlockSpec((nh, N_KEYS, tm), lambda i, e: (0, 0, i), pipeline_mode=once),
            pl.BlockSpec((nh, N_KEYS, tm), lambda i, e: (0, 0, i), pipeline_mode=once),
            pl.BlockSpec((nh, 8, tm), lambda i, e: (0, 0, i)),
            pl.BlockSpec((tm, D_MODEL), lambda i, e: (i, 0), pipeline_mode=once),
        ],
        out_specs=pl.BlockSpec((tm, D_MODEL), lambda i, e: (i, 0)),
        out_shape=jax.ShapeDtypeStruct((t, D_MODEL), F32),
        compiler_params=_cparams("parallel", "arbitrary"),
        name="peer_dense",
    )(xn, peer_u, peer_v, s1t, p1t, s2t, p2t, tau, x_res)


def kernel(x_prompt, x_sample, mem_prompt, cache_k, cache_v, cache_mem_k, cache_mem_v, state_ssm_re, state_ssm_im, state_pool, page_table, norm_mix, w_in, ssm_a_re, ssm_a_im, ssm_log_dt, ssm_b_re, ssm_b_im, ssm_c_re, ssm_c_im, ssm_d, ssm_w_glu, pool_w, pool_scale, att_lq1, att_lk1, att_lq2, att_lk2, att_subln, w_up_ssm, w_up_pool, w_up_att, w_out, norm_cross, norm_mem, w_mem_q, w_mem_k, w_mem_v, w_mem_o, norm_ffn, peer_w_query, peer_keys1, peer_keys2, peer_u, peer_v, norm_final):
    n_p, seq, d = x_prompt.shape
    n_s, dec_seq, _ = x_sample.shape
    depth = w_in.shape[0]
    tp = n_p * seq
    ts = n_s * dec_seq
    past_len = page_table.shape[1] * cache_k.shape[2]

    tm = 768
    tm_mm = 384
    tc_p, tc_s = 128, dec_seq

    x = jnp.concatenate([x_prompt.reshape(tp, d), x_sample.reshape(ts, d)], axis=0)
    mem_rows = mem_prompt.reshape(n_p * N_MEM, d)
    cmk = cache_mem_k.reshape(depth, n_s, N_MEM, d)
    cmv = cache_mem_v.reshape(depth, n_s, N_MEM, d)
    zeros_h = jnp.zeros((n_p, N_STATE), F32)
    zeros_buf = jnp.zeros((n_p, POOL_CARRY, W_POOL), F32)
    sample_buf = jnp.pad(state_pool, ((0, 0), (0, 0), (POOL_CARRY - POOL_BUF, 0), (0, 0)))
    (w_in, w_up_ssm, w_up_pool, w_up_att, w_out, w_mem_q, w_mem_k, w_mem_v, w_mem_o,
     peer_w_query, peer_u, peer_v) = (
        w.astype(BF16) for w in (w_in, w_up_ssm, w_up_pool, w_up_att, w_out, w_mem_q, w_mem_k,
                                 w_mem_v, w_mem_o, peer_w_query, peer_u, peer_v))

    outs = {k: [] for k in ("kp", "vp", "mkp", "mvp", "hrp", "hip", "bp", "ks", "vs", "hrs", "his", "bs")}
    for l in range(depth):
        a_mat = _in_proj(x, norm_mix[l], w_in, l, 0, COL_GATE, gate=False, tm=tm, tn=1024)
        gates = _in_proj(x, norm_mix[l], w_in, l, COL_GATE, 3 * D_MODEL, gate=True, tm=tm, tn=1024)
        k_all = a_mat[:, COL_K:COL_V]
        v_all = a_mat[:, COL_V:COL_GATE]
        outs["kp"].append(k_all[:tp].reshape(n_p, seq, N_HEADS_ATT, W_HEAD))
        outs["vp"].append(v_all[:tp].reshape(n_p, seq, N_HEADS_ATT, W_HEAD))
        outs["ks"].append(k_all[tp:].reshape(n_s, dec_seq, N_HEADS_ATT, W_HEAD))
        outs["vs"].append(v_all[tp:].reshape(n_s, dec_seq, N_HEADS_ATT, W_HEAD))

        s5_tab = _s5_tables(ssm_a_re[l], ssm_a_im[l], ssm_log_dt[l], ssm_b_re[l], ssm_b_im[l],
                            ssm_c_re[l], ssm_c_im[l])
        z_p, hrp, hip = _s5(a_mat, 0, n_p, seq, s5_tab, ssm_d[l],
                            zeros_h, zeros_h, ssm_w_glu, l, tc=tc_p)
        z_s, hrs, his = _s5(a_mat, tp, n_s, dec_seq, s5_tab, ssm_d[l],
                            state_ssm_re[l].reshape(n_s, N_STATE), state_ssm_im[l].reshape(n_s, N_STATE),
                            ssm_w_glu, l, tc=tc_s)
        outs["hrp"].append(hrp); outs["hip"].append(hip)
        outs["hrs"].append(hrs); outs["his"].append(his)

        yb_p, bp = _pool(a_mat, 0, n_p, seq, zeros_buf, pool_w, pool_scale[l], l, 0, tc=tc_p)
        yb_s, bs = _pool(a_mat, tp, n_s, dec_seq, sample_buf[l], pool_w, pool_scale[l], l, past_len, tc=tc_s)
        outs["bp"].append(bp); outs["bs"].append(bs)

        lam_init = 0.8 - 0.6 * math.exp(-0.3 * l)
        lam = (jnp.exp(jnp.sum(att_lq1[l] * att_lk1[l])) - jnp.exp(jnp.sum(att_lq2[l] * att_lk2[l])) + lam_init)
        lam_row = jnp.full((8, W_HEAD), lam, F32)
        yc_p = _attn_prompt(a_mat, n_p, seq, lam_row, att_subln[l], lam_init, tq=512, hb=2)
        yc_s = _attn_sample(a_mat, tp, n_s, dec_seq, cache_k, cache_v, page_table, lam_row,
                            att_subln[l], lam_init, l, pages_per_step=8)

        z = jnp.concatenate([z_p, z_s], axis=0)
        yb = jnp.concatenate([yb_p, yb_s], axis=0)
        yc = jnp.concatenate([yc_p, yc_s], axis=0)
        merged = _merge(z, yb, yc, gates, w_up_ssm, w_up_pool, w_up_att, l, tm=tm, tn=512)
        x = _matmul(merged, w_out, l, res=x, tm=tm_mm, name="w_out")

        mk_p = _matmul(mem_rows, w_mem_k, l, gain=norm_mem[l], tm=N_MEM, name="mem_k")
        mv_p = _matmul(mem_rows, w_mem_v, l, gain=norm_mem[l], tm=N_MEM, name="mem_v")
        outs["mkp"].append(mk_p.reshape(n_p, N_MEM, N_HEADS_MEM, HEAD_DIM_MEM))
        outs["mvp"].append(mv_p.reshape(n_p, N_MEM, N_HEADS_MEM, HEAD_DIM_MEM))
        qm = _matmul(x, w_mem_q, l, gain=norm_cross[l], tm=tm_mm, name="mem_q")
        ca_p = _cross_attn(qm, 0, n_p, seq, mk_p.reshape(n_p, N_MEM, d), mv_p.reshape(n_p, N_MEM, d),
                           lambda n: (n, 0, 0), tq=512)
        ca_s = _cross_attn(qm, tp, n_s, dec_seq, cmk, cmv, lambda n: (l, n, 0, 0), tq=dec_seq)
        x = _matmul(jnp.concatenate([ca_p, ca_s], axis=0), w_mem_o, l, res=x, tm=tm_mm, name="mem_o")

        xn, *score = _peer_score(x, norm_ffn[l], peer_w_query, peer_keys1, peer_keys2, l, tm=tm)
        x = _peer_dense(xn, x, score, peer_u, peer_v, l, tm=tm, te=512)

    y = _final_norm(x, norm_final, tm=tm)
    st = lambda k: jnp.stack(outs[k])
    return (y[:tp].reshape(n_p, seq, d), y[tp:].reshape(n_s, dec_seq, d),
            st("kp"), st("vp"), st("mkp"), st("mvp"), st("hrp"), st("hip"), st("bp"),
            st("ks"), st("vs"), st("hrs"), st("his"), st("bs"))
```

```python
import functools
import math

import jax
import jax.numpy as jnp
from jax import lax
from jax.experimental import pallas as pl
from jax.experimental.pallas import tpu as pltpu

F32 = jnp.float32
BF16 = jnp.bfloat16
EPS = 1e-6
NEG_INF = -1e30
LOG2E = math.log2(math.e)

D_MODEL = 2048
W_SSM = 512
SSM_GROUP = 16
N_SSM_GROUPS = 32
SSM_STATE = 64
N_STATE = N_SSM_GROUPS * SSM_STATE
W_POOL = 512
POOL_WINDOWS = (2, 4, 8, 16)
POOL_GROUP = 128
POOL_BUF = 15
POOL_CARRY = 16
SCAN_GROUP = 8
N_HEADS_ATT = 8
HEAD_DIM_ATT = 64
W_HEAD = 2 * HEAD_DIM_ATT
W_ATT = N_HEADS_ATT * W_HEAD
N_MEM = 256
N_HEADS_MEM = 4
HEAD_DIM_MEM = 512
N_KEYS = 128
N_RETR_HEADS = 8
D_QUERY = 256
TOPK = 16

COL_SSM = 0
COL_POOL = W_SSM
COL_Q = W_SSM + W_POOL
COL_K = COL_Q + W_ATT
COL_V = COL_K + W_ATT
COL_GATE = COL_V + W_ATT
IN_COLS = COL_GATE + 3 * D_MODEL

VMEM_LIMIT_BYTES = 56 * 1024 * 1024
MXU_DEPTH = 256
NT_DIMS = (((1,), (1,)), ((), ()))


def _cparams(*sem):
    return pltpu.CompilerParams(dimension_semantics=sem, vmem_limit_bytes=VMEM_LIMIT_BYTES)


def _bdot(a, b):
    return jnp.dot(a.astype(BF16), b.astype(BF16), preferred_element_type=F32)


def _bdot_nt(a, b):
    return lax.dot_general(a.astype(BF16), b.astype(BF16), NT_DIMS, preferred_element_type=F32)


def _rms(x, g):
    return x * lax.rsqrt(jnp.mean(x * x, axis=-1, keepdims=True) + EPS) * g


def _without_ref(fn, idx):
    def wrapped(*refs):
        return fn(*refs[:idx], *refs[idx + 1:])
    return wrapped


_DST_SPEC = pl.BlockSpec(memory_space=pl.ANY)


def _gelu(x):
    return 0.5 * x * (1.0 + lax.erf(x * (2.0 ** -0.5)))


def _in_proj_kernel(x_ref, g_ref, w_ref, o_ref, xn_ref):
    @pl.when(pl.program_id(1) == 0)
    def _():
        xn_ref[...] = _rms(x_ref[...], g_ref[...]).astype(BF16)

    o_ref[...] = jnp.dot(xn_ref[...], w_ref[...], preferred_element_type=F32)


def _in_proj(x, g, w_in, l, n_cols, *, tm, tn):
    t = x.shape[0]
    return pl.pallas_call(
        _in_proj_kernel,
        grid=(t // tm, n_cols // tn),
        in_specs=[
            pl.BlockSpec((tm, D_MODEL), lambda i, j: (i, 0)),
            pl.BlockSpec((1, D_MODEL), lambda i, j: (0, 0)),
            pl.BlockSpec((None, D_MODEL, tn), lambda i, j: (l, 0, j)),
        ],
        out_specs=pl.BlockSpec((tm, tn), lambda i, j: (i, j)),
        out_shape=jax.ShapeDtypeStruct((t, n_cols), F32),
        scratch_shapes=[pltpu.VMEM((tm, D_MODEL), BF16)],
        compiler_params=_cparams("parallel", "arbitrary"),
        name="in_proj",
    )(x, g.reshape(1, D_MODEL), w_in)


def _mm_kernel(*refs, has_norm, has_res):
    x_ref = refs[0]
    pos = 1
    g_ref = None
    if has_norm:
        g_ref = refs[pos]
        pos += 1
    w_ref = refs[pos]
    pos += 1
    r_ref = None
    if has_res:
        r_ref = refs[pos]
        pos += 1
    o_ref = refs[pos]
    x = x_ref[...]
    if has_norm:
        x = _rms(x, g_ref[...])
    res = jnp.dot(x.astype(BF16), w_ref[...], preferred_element_type=F32)
    if has_res:
        res = res + r_ref[...]
    o_ref[...] = res


def _matmul(x, w, l, *, gain=None, res=None, tm, name):
    t, k = x.shape
    n = w.shape[2]
    has_norm = gain is not None
    has_res = res is not None
    in_specs = [pl.BlockSpec((tm, k), lambda i: (i, 0))]
    args = [x]
    if has_norm:
        in_specs.append(pl.BlockSpec((1, k), lambda i: (0, 0)))
        args.append(gain.reshape(1, k))
    in_specs.append(pl.BlockSpec((None, k, n), lambda i: (l, 0, 0), pipeline_mode=pl.Buffered(1)))
    args.append(w)
    if has_res:
        in_specs.append(pl.BlockSpec((tm, n), lambda i: (i, 0)))
        args.append(res)
    return pl.pallas_call(
        functools.partial(_mm_kernel, has_norm=has_norm, has_res=has_res),
        grid=(t // tm,),
        in_specs=in_specs,
        out_specs=pl.BlockSpec((tm, n), lambda i: (i, 0)),
        out_shape=jax.ShapeDtypeStruct((t, n), F32),
        compiler_params=_cparams("parallel"),
        name=name,
    )(*args)


def _gate_merge_kernel(x_ref, g_ref, z_ref, yb_ref, yc_ref, wg0_ref, wg1_ref, wg2_ref,
                       wa_ref, wb_ref, wc_ref, o_ref, xn_ref):
    @pl.when(pl.program_id(1) == 0)
    def _():
        xn_ref[...] = _rms(x_ref[...], g_ref[...]).astype(BF16)

    xn = xn_ref[...]
    acc = None
    for y_ref, wg_ref, wu_ref in ((z_ref, wg0_ref, wa_ref), (yb_ref, wg1_ref, wb_ref),
                                  (yc_ref, wg2_ref, wc_ref)):
        gate = jax.nn.sigmoid(jnp.dot(xn, wg_ref[...], preferred_element_type=F32))
        term = gate * _bdot(y_ref[...], wu_ref[...])
        acc = term if acc is None else acc + term
    o_ref[...] = acc


def _gate_merge(x, g, z, yb, yc, w_in, w_up_ssm, w_up_pool, w_up_att, l, *, tm, tn):
    t = x.shape[0]
    nb = D_MODEL // tn
    gb0 = COL_GATE // tn
    gate_spec = lambda b: pl.BlockSpec((None, D_MODEL, tn), lambda i, j: (l, 0, gb0 + b * nb + j))
    return pl.pallas_call(
        _gate_merge_kernel,
        grid=(t // tm, nb),
        in_specs=[
            pl.BlockSpec((tm, D_MODEL), lambda i, j: (i, 0)),
            pl.BlockSpec((1, D_MODEL), lambda i, j: (0, 0)),
            pl.BlockSpec((tm, W_SSM), lambda i, j: (i, 0)),
            pl.BlockSpec((tm, W_POOL), lambda i, j: (i, 0)),
            pl.BlockSpec((tm, W_ATT), lambda i, j: (i, 0)),
            gate_spec(0), gate_spec(1), gate_spec(2),
            pl.BlockSpec((None, W_SSM, tn), lambda i, j: (l, 0, j)),
            pl.BlockSpec((None, W_POOL, tn), lambda i, j: (l, 0, j)),
            pl.BlockSpec((None, W_ATT, tn), lambda i, j: (l, 0, j)),
        ],
        out_specs=pl.BlockSpec((tm, tn), lambda i, j: (i, j)),
        out_shape=jax.ShapeDtypeStruct((t, D_MODEL), F32),
        scratch_shapes=[pltpu.VMEM((tm, D_MODEL), BF16)],
        compiler_params=_cparams("parallel", "arbitrary"),
        name="gate_merge",
    )(x, g.reshape(1, D_MODEL), z, yb, yc, w_in, w_in, w_in, w_up_ssm, w_up_pool, w_up_att)


def _final_norm_kernel(x_ref, g_ref, o_ref):
    o_ref[...] = _rms(x_ref[...], g_ref[...])


def _final_norm(x, g, *, tm):
    t = x.shape[0]
    return pl.pallas_call(
        _final_norm_kernel,
        grid=(t // tm,),
        in_specs=[pl.BlockSpec((tm, D_MODEL), lambda i: (i, 0)),
                  pl.BlockSpec((1, D_MODEL), lambda i: (0, 0))],
        out_specs=pl.BlockSpec((tm, D_MODEL), lambda i: (i, 0)),
        out_shape=jax.ShapeDtypeStruct((t, D_MODEL), F32),
        compiler_params=_cparams("parallel"),
        name="final_norm",
    )(x, g.reshape(1, D_MODEL))


def _s5_tables(a_re, a_im, log_dt, b_re, b_im, c_re, c_im):
    g_, p_ = N_SSM_GROUPS, SSM_STATE
    dt = jnp.exp(log_dt)[:, None]
    mag = jnp.exp(a_re * dt)
    lb_re = mag * jnp.cos(a_im * dt)
    lb_im = mag * jnp.sin(a_im * dt)
    den = a_re * a_re + a_im * a_im
    nr = lb_re - 1.0
    f_re = (nr * a_re + lb_im * a_im) / den
    f_im = (lb_im * a_re - nr * a_im) / den
    bb_re = f_re[..., None] * b_re - f_im[..., None] * b_im
    bb_im = f_re[..., None] * b_im + f_im[..., None] * b_re
    eye = jnp.eye(g_, dtype=F32)
    bd_re = jnp.einsum('gpc,gh->gchp', bb_re, eye).reshape(W_SSM, N_STATE)
    bd_im = jnp.einsum('gpc,gh->gchp', bb_im, eye).reshape(W_SSM, N_STATE)
    bmat = jnp.concatenate([bd_re, bd_im], axis=1).astype(BF16)
    cd_re = jnp.einsum('gcp,gh->gphc', c_re, eye).reshape(N_STATE, W_SSM).astype(BF16)
    cd_im = jnp.einsum('gcp,gh->gphc', -c_im, eye).reshape(N_STATE, W_SSM).astype(BF16)
    lr = lb_re.reshape(1, N_STATE)
    li = lb_im.reshape(1, N_STATE)
    nsteps = int(math.log2(SCAN_GROUP))
    sq_re, sq_im = [lr], [li]
    for _ in range(nsteps):
        r, i = sq_re[-1], sq_im[-1]
        sq_re.append(r * r - i * i)
        sq_im.append(2.0 * r * i)
    pt_re, pt_im = lr, li
    for s in range(nsteps):
        r, i = sq_re[s], sq_im[s]
        pt_re, pt_im = (jnp.concatenate([pt_re, pt_re * r - pt_im * i], axis=0),
                        jnp.concatenate([pt_im, pt_re * i + pt_im * r], axis=0))
    pw_re = jnp.concatenate(sq_re[:nsteps], axis=0)
    pw_im = jnp.concatenate(sq_im[:nsteps], axis=0)
    return bmat, cd_re, cd_im, pw_re, pw_im, pt_re, pt_im


def _s5_kernel(u_ref, b_ref, cre_ref, cim_ref, d_ref, pwr_ref, pwi_ref, ptr_ref, pti_ref,
               h0r_ref, h0i_ref, wglu_ref, z_ref, hr_ref, hi_ref, cr_ref, ci_ref, *, tc, nsteps):
    c = pl.program_id(1)

    @pl.when(c == 0)
    def _():
        cr_ref[...] = h0r_ref[...]
        ci_ref[...] = h0i_ref[...]

    u = u_ref[...]
    bu = jnp.dot(u.astype(BF16), b_ref[...], preferred_element_type=F32)
    re = bu[:, :N_STATE]
    im = bu[:, N_STATE:]
    row = lax.broadcasted_iota(jnp.int32, (tc, 1), 0) % SCAN_GROUP
    for s in range(nsteps):
        k = 1 << s
        keep = row >= k
        sre = jnp.where(keep, pltpu.roll(re, k, 0), 0.0)
        sim = jnp.where(keep, pltpu.roll(im, k, 0), 0.0)
        ar = pwr_ref[s:s + 1, :]
        ai = pwi_ref[s:s + 1, :]
        re, im = re + (ar * sre - ai * sim), im + (ar * sim + ai * sre)
    hr = cr_ref[...]
    hi = ci_ref[...]
    pr = ptr_ref[...]
    pi = pti_ref[...]
    res, ims = [], []
    for j in range(0, tc, SCAN_GROUP):
        gr = re[j:j + SCAN_GROUP] + (pr * hr - pi * hi)
        gi = im[j:j + SCAN_GROUP] + (pr * hi + pi * hr)
        hr = gr[SCAN_GROUP - 1:SCAN_GROUP, :]
        hi = gi[SCAN_GROUP - 1:SCAN_GROUP, :]
        res.append(gr)
        ims.append(gi)
    re = jnp.concatenate(res, axis=0)
    im = jnp.concatenate(ims, axis=0)
    cr_ref[...] = hr
    ci_ref[...] = hi
    y = (jnp.dot(re.astype(BF16), cre_ref[...], preferred_element_type=F32)
         + jnp.dot(im.astype(BF16), cim_ref[...], preferred_element_type=F32)
         + d_ref[...] * u)
    z = _gelu(y)
    z_ref[...] = z * jax.nn.sigmoid(_bdot(z, wglu_ref[...]))

    @pl.when(c == pl.num_programs(1) - 1)
    def _():
        hr_ref[...] = re[tc - 1:tc, :]
        hi_ref[...] = im[tc - 1:tc, :]


def _s5(a_mat, row0, n_seq, seq_len, tables, d_skip, h0_re, h0_im, w_glu, l, *, tc, dst=None):
    bmat, cd_re, cd_im, pw_re, pw_im, pt_re, pt_im = tables
    nsteps = pw_re.shape[0]
    assert tc % SCAN_GROUP == 0
    nc = seq_len // tc
    rb0 = row0 // tc
    const = lambda shape: pl.BlockSpec(shape, lambda n, c: tuple(0 for _ in shape))
    kern = functools.partial(_s5_kernel, tc=tc, nsteps=nsteps)
    n_in = 12
    z, hr, hi = pl.pallas_call(
        kern if dst is None else _without_ref(kern, n_in),
        grid=(n_seq, nc),
        input_output_aliases={} if dst is None else {n_in: 0},
        in_specs=[
            pl.BlockSpec((tc, W_SSM), lambda n, c: (rb0 + n * nc + c, COL_SSM // W_SSM)),
            const((W_SSM, 2 * N_STATE)),
            const((N_STATE, W_SSM)),
            const((N_STATE, W_SSM)),
            const((1, W_SSM)),
            const(pw_re.shape),
            const(pw_im.shape),
            const((SCAN_GROUP, N_STATE)),
            const((SCAN_GROUP, N_STATE)),
            pl.BlockSpec((None, 1, N_STATE), lambda n, c: (n, 0, 0)),
            pl.BlockSpec((None, 1, N_STATE), lambda n, c: (n, 0, 0)),
            pl.BlockSpec((None, W_SSM, W_SSM), lambda n, c: (l, 0, 0)),
        ] + ([] if dst is None else [_DST_SPEC]),
        out_specs=[
            pl.BlockSpec((tc, W_SSM), lambda n, c: (rb0 + n * nc + c, 0)),
            pl.BlockSpec((None, 1, N_STATE), lambda n, c: (n, 0, 0)),
            pl.BlockSpec((None, 1, N_STATE), lambda n, c: (n, 0, 0)),
        ],
        out_shape=[jax.ShapeDtypeStruct((a_mat.shape[0], W_SSM), F32),
                   jax.ShapeDtypeStruct((n_seq, 1, N_STATE), F32),
                   jax.ShapeDtypeStruct((n_seq, 1, N_STATE), F32)],
        scratch_shapes=[pltpu.VMEM((1, N_STATE), F32), pltpu.VMEM((1, N_STATE), F32)],
        compiler_params=_cparams("parallel", "arbitrary"),
        name="s5",
    )(a_mat, bmat, cd_re, cd_im, d_skip.reshape(1, W_SSM), pw_re, pw_im, pt_re, pt_im,
      h0_re.reshape(n_seq, 1, N_STATE), h0_im.reshape(n_seq, 1, N_STATE), w_glu,
      *(() if dst is None else (dst,)))
    shp = (n_seq, N_SSM_GROUPS, SSM_STATE)
    return z, hr.reshape(shp), hi.reshape(shp)


def _pool_kernel(u_ref, buf_ref, w_ref, sc_ref, y_ref, nb_ref, xx_ref, *, tc, start_pos):
    c = pl.program_id(1)

    @pl.when(c == 0)
    def _():
        xx_ref[0:POOL_CARRY, :] = buf_ref[...]

    u = u_ref[...]
    xx_ref[POOL_CARRY:, :] = u
    x = xx_ref[...]
    sums = []
    s = x
    for w in (1, 2, 4, 8):
        s = s + pltpu.roll(s, w, 0)
        sums.append(s)
    t_abs = (start_pos + c * tc + lax.broadcasted_iota(jnp.int32, (tc, 1), 0) + 1).astype(F32)
    outs = []
    for gi, w in enumerate(POOL_WINDOWS):
        sl = slice(gi * POOL_GROUP, (gi + 1) * POOL_GROUP)
        mean = sums[gi][POOL_CARRY:, sl] / jnp.minimum(t_abs, float(w))
        d = mean - u[:, sl]
        outs.append(_bdot(d, w_ref[gi]))
    y_ref[...] = jnp.concatenate(outs, axis=-1) * sc_ref[...]
    carry = x[tc:, :]
    xx_ref[0:POOL_CARRY, :] = carry

    @pl.when(c == pl.num_programs(1) - 1)
    def _():
        nb_ref[...] = carry


def _pool(a_mat, row0, n_seq, seq_len, buf16, pool_w, pool_scale, l, start_pos, *, tc, dst=None):
    nc = seq_len // tc
    rb0 = row0 // tc
    kern = functools.partial(_pool_kernel, tc=tc, start_pos=start_pos)
    n_in = 4
    y, nb = pl.pallas_call(
        kern if dst is None else _without_ref(kern, n_in),
        grid=(n_seq, nc),
        input_output_aliases={} if dst is None else {n_in: 0},
        in_specs=[
            pl.BlockSpec((tc, W_POOL), lambda n, c: (rb0 + n * nc + c, COL_POOL // W_POOL)),
            pl.BlockSpec((None, POOL_CARRY, W_POOL), lambda n, c: (n, 0, 0)),
            pl.BlockSpec((None, len(POOL_WINDOWS), POOL_GROUP, POOL_GROUP), lambda n, c: (l, 0, 0, 0)),
            pl.BlockSpec((1, W_POOL), lambda n, c: (0, 0)),
        ] + ([] if dst is None else [_DST_SPEC]),
        out_specs=[
            pl.BlockSpec((tc, W_POOL), lambda n, c: (rb0 + n * nc + c, 0)),
            pl.BlockSpec((None, POOL_CARRY, W_POOL), lambda n, c: (n, 0, 0)),
        ],
        out_shape=[jax.ShapeDtypeStruct((a_mat.shape[0], W_POOL), F32),
                   jax.ShapeDtypeStruct((n_seq, POOL_CARRY, W_POOL), F32)],
        scratch_shapes=[pltpu.VMEM((POOL_CARRY + tc, W_POOL), F32)],
        compiler_params=_cparams("parallel", "arbitrary"),
        name="pool",
    )(a_mat, buf16, pool_w, pool_scale.reshape(1, W_POOL), *(() if dst is None else (dst,)))
    return y, nb[:, POOL_CARRY - POOL_BUF:, :]


def _subln(o, g, lam_init):
    return _rms(o, g) * (1.0 - lam_init)


def _attn_prompt_kernel(qt_ref, kt_ref, q_ref, k_ref, v_ref, lam_ref, g_ref, o_ref,
                        m_ref, l_ref, acc_ref, *, tq, hb, lam_init):
    p = pl.program_id(2)
    qi = qt_ref[p]
    ki = kt_ref[p]

    @pl.when(ki == 0)
    def _():
        m_ref[...] = jnp.full(m_ref.shape, NEG_INF, F32)
        l_ref[...] = jnp.zeros(l_ref.shape, F32)
        acc_ref[...] = jnp.zeros(acc_ref.shape, F32)

    def accumulate(masked):
        q = (q_ref[...] * (HEAD_DIM_ATT ** -0.5 * LOG2E)).astype(BF16)
        k = k_ref[...].astype(BF16)
        v = v_ref[...].astype(BF16)
        if masked:
            causal = (lax.broadcasted_iota(jnp.int32, (tq, tq), 0)
                      >= lax.broadcasted_iota(jnp.int32, (tq, tq), 1))
        for hh in range(hb):
            vh = v[:, hh * W_HEAD:(hh + 1) * W_HEAD]
            for c in range(2):
                idx = 2 * hh + c
                sl = slice(hh * W_HEAD + c * HEAD_DIM_ATT, hh * W_HEAD + (c + 1) * HEAD_DIM_ATT)
                s = lax.dot_general(q[:, sl], k[:, sl], NT_DIMS, preferred_element_type=F32)
                if masked:
                    s = jnp.where(causal, s, NEG_INF)
                m_prev = m_ref[idx]
                m_new = jnp.maximum(m_prev, jnp.max(s, axis=-1, keepdims=True))
                alpha = jnp.exp2(m_prev - m_new)
                pexp = jnp.exp2(s - jnp.concatenate([m_new] * (tq // W_HEAD), axis=1))
                l_ref[idx] = alpha * l_ref[idx] + jnp.sum(pexp, axis=-1, keepdims=True)
                acc_ref[idx] = alpha * acc_ref[idx] + jnp.dot(pexp.astype(BF16), vh,
                                                              preferred_element_type=F32)
                m_ref[idx] = m_new

    @pl.when(ki < qi)
    def _():
        accumulate(False)

    @pl.when(ki == qi)
    def _():
        accumulate(True)
        lam = lam_ref[0:1, :]
        for hh in range(hb):
            o = (acc_ref[2 * hh] / l_ref[2 * hh]
                 - lam * (acc_ref[2 * hh + 1] / l_ref[2 * hh + 1]))
            o_ref[:, hh * W_HEAD:(hh + 1) * W_HEAD] = _subln(o, g_ref[...], lam_init)


def _attn_prompt(a_mat, n_seq, seq_len, lam_row, subln_g, lam_init, *, tq, hb):
    nq = seq_len // tq
    pairs = [(qi, ki) for qi in range(nq) for ki in range(qi + 1)]
    qt = jnp.asarray([p[0] for p in pairs], jnp.int32)
    kt = jnp.asarray([p[1] for p in pairs], jnp.int32)
    wb = hb * W_HEAD
    cq, ck, cv = COL_Q // wb, COL_K // wb, COL_V // wb
    grid_spec = pltpu.PrefetchScalarGridSpec(
        num_scalar_prefetch=2,
        grid=(n_seq, N_HEADS_ATT // hb, len(pairs)),
        in_specs=[
            pl.BlockSpec((tq, wb), lambda b, h, p, qt, kt: (b * nq + qt[p], cq + h)),
            pl.BlockSpec((tq, wb), lambda b, h, p, qt, kt: (b * nq + kt[p], ck + h)),
            pl.BlockSpec((tq, wb), lambda b, h, p, qt, kt: (b * nq + kt[p], cv + h)),
            pl.BlockSpec((8, W_HEAD), lambda b, h, p, qt, kt: (0, 0)),
            pl.BlockSpec((1, W_HEAD), lambda b, h, p, qt, kt: (0, 0)),
        ],
        out_specs=pl.BlockSpec((tq, wb), lambda b, h, p, qt, kt: (b * nq + qt[p], h)),
        scratch_shapes=[pltpu.VMEM((2 * hb, tq, W_HEAD), F32), pltpu.VMEM((2 * hb, tq, W_HEAD), F32),
                        pltpu.VMEM((2 * hb, tq, W_HEAD), F32)],
    )
    return pl.pallas_call(
        functools.partial(_attn_prompt_kernel, tq=tq, hb=hb, lam_init=lam_init),
        grid_spec=grid_spec,
        out_shape=jax.ShapeDtypeStruct((a_mat.shape[0], W_ATT), F32),
        compiler_params=_cparams("parallel", "parallel", "arbitrary"),
        name="attn_prompt",
    )(qt, kt, a_mat, a_mat, a_mat, lam_row, subln_g.reshape(1, W_HEAD))


def _attn_sample_kernel(pt_ref, q_ref, kn_ref, vn_ref, lam_ref, g_ref, *rest, n_pages, dec_seq, lam_init):
    k_refs = rest[:n_pages]
    v_refs = rest[n_pages:2 * n_pages]
    o_ref, qh_ref, bias_ref, m_ref, l_ref, acc_ref = rest[2 * n_pages:]
    step = pl.program_id(1)
    rph = 2 * dec_seq

    @pl.when(step == 0)
    def _():
        q = q_ref[...] * (HEAD_DIM_ATT ** -0.5)
        row = lax.broadcasted_iota(jnp.int32, (rph, W_HEAD), 0)
        lane = lax.broadcasted_iota(jnp.int32, (rph, W_HEAD), 1)
        keep = lane // HEAD_DIM_ATT == row // dec_seq
        for h in range(N_HEADS_ATT):
            qh = q[:, h * W_HEAD:(h + 1) * W_HEAD]
            qh_ref[h] = jnp.where(keep, jnp.concatenate([qh, qh], axis=0), 0.0).astype(BF16)
        m_ref[...] = jnp.full(m_ref.shape, NEG_INF, F32)
        l_ref[...] = jnp.zeros(l_ref.shape, F32)
        acc_ref[...] = jnp.zeros(acc_ref.shape, F32)
        brow = lax.broadcasted_iota(jnp.int32, bias_ref.shape, 0) // rph
        bcol = lax.broadcasted_iota(jnp.int32, bias_ref.shape, 1) % N_HEADS_ATT
        bias_ref[...] = jnp.where(brow == bcol, 0.0, NEG_INF)

    def update(s, pv_of):
        m_prev = m_ref[...]
        m_new = jnp.maximum(m_prev, jnp.max(s, axis=-1, keepdims=True))
        alpha = jnp.exp(m_prev - m_new)
        pexp = jnp.exp(s - m_new)
        l_ref[...] = alpha * l_ref[...] + jnp.sum(pexp, axis=-1, keepdims=True)
        acc_ref[...] = alpha * acc_ref[...] + pv_of(pexp.astype(BF16))
        m_ref[...] = m_new

    q_all = jnp.concatenate([qh_ref[h] for h in range(N_HEADS_ATT)], axis=0)
    k_all = jnp.concatenate([r[...].astype(BF16) for r in k_refs], axis=0)
    v_all = jnp.concatenate([r[...].astype(BF16) for r in v_refs], axis=0)
    s = lax.dot_general(q_all, k_all, NT_DIMS, preferred_element_type=F32)
    s = s + jnp.concatenate([bias_ref[...]] * n_pages, axis=1)
    update(s, lambda pb: jnp.dot(pb, v_all, preferred_element_type=F32))

    @pl.when(step == pl.num_programs(1) - 1)
    def _():
        pad = jnp.zeros((W_HEAD - dec_seq, W_HEAD), F32)

        def new_rows(ref, h):
            return jnp.concatenate([ref[:, h * W_HEAD:(h + 1) * W_HEAD], pad], axis=0).astype(BF16)

        s2 = jnp.concatenate(
            [lax.dot_general(qh_ref[h], new_rows(kn_ref, h), NT_DIMS, preferred_element_type=F32)
             for h in range(N_HEADS_ATT)], axis=0)
        qi = lax.broadcasted_iota(jnp.int32, s2.shape, 0) % dec_seq
        kj = lax.broadcasted_iota(jnp.int32, s2.shape, 1)
        s2 = jnp.where(qi >= kj, s2, NEG_INF)
        update(s2, lambda pb: jnp.concatenate(
            [jnp.dot(pb[h * rph:(h + 1) * rph], new_rows(vn_ref, h), preferred_element_type=F32)
             for h in range(N_HEADS_ATT)], axis=0))
        o = acc_ref[...] / l_ref[...]
        lam = lam_ref[0:1, :]
        for h in range(N_HEADS_ATT):
            r0 = h * rph
            oh = o[r0:r0 + dec_seq] - lam * o[r0 + dec_seq:r0 + rph]
            o_ref[:, h * W_HEAD:(h + 1) * W_HEAD] = _subln(oh, g_ref[...], lam_init)


def _attn_sample(a_mat, row0, n_seq, dec_seq, cache_k, cache_v, page_table, lam_row, subln_g,
                 lam_init, l, dst, *, pages_per_step):
    depth, n_pool, page, _, _ = cache_k.shape
    ck = cache_k.reshape(depth, n_pool, page * N_HEADS_ATT, W_HEAD)
    cv = cache_v.reshape(depth, n_pool, page * N_HEADS_ATT, W_HEAD)
    n_pt = page_table.shape[1]
    r = pages_per_step
    rb0 = row0 // dec_seq
    cq, ckk, cvv = COL_Q // W_ATT, COL_K // W_ATT, COL_V // W_ATT

    def page_spec(j):
        return pl.BlockSpec((None, None, page * N_HEADS_ATT, W_HEAD),
                            lambda n, s, pt: (l, pt[n, s * r + j], 0, 0))

    grid_spec = pltpu.PrefetchScalarGridSpec(
        num_scalar_prefetch=1,
        grid=(n_seq, n_pt // r),
        in_specs=[
            pl.BlockSpec((dec_seq, W_ATT), lambda n, s, pt: (rb0 + n, cq)),
            pl.BlockSpec((dec_seq, W_ATT), lambda n, s, pt: (rb0 + n, ckk)),
            pl.BlockSpec((dec_seq, W_ATT), lambda n, s, pt: (rb0 + n, cvv)),
            pl.BlockSpec((8, W_HEAD), lambda n, s, pt: (0, 0)),
            pl.BlockSpec((1, W_HEAD), lambda n, s, pt: (0, 0)),
        ] + [page_spec(j) for j in range(r)] + [page_spec(j) for j in range(r)] + [_DST_SPEC],
        out_specs=pl.BlockSpec((dec_seq, W_ATT), lambda n, s, pt: (rb0 + n, 0)),
        scratch_shapes=[pltpu.VMEM((N_HEADS_ATT, 2 * dec_seq, W_HEAD), BF16),
                        pltpu.VMEM((2 * N_HEADS_ATT * dec_seq, page * N_HEADS_ATT), F32),
                        pltpu.VMEM((2 * N_HEADS_ATT * dec_seq, 1), F32),
                        pltpu.VMEM((2 * N_HEADS_ATT * dec_seq, 1), F32),
                        pltpu.VMEM((2 * N_HEADS_ATT * dec_seq, W_HEAD), F32)],
    )
    n_in = 5 + 2 * r
    kern = functools.partial(_attn_sample_kernel, n_pages=r, dec_seq=dec_seq, lam_init=lam_init)
    return pl.pallas_call(
        _without_ref(kern, 1 + n_in),
        grid_spec=grid_spec,
        input_output_aliases={1 + n_in: 0},
        out_shape=jax.ShapeDtypeStruct(dst.shape, F32),
        compiler_params=_cparams("parallel", "arbitrary"),
        name="attn_sample",
    )(page_table, a_mat, a_mat, a_mat, lam_row, subln_g.reshape(1, W_HEAD),
      *([ck] * r), *([cv] * r), dst)


def _cross_kernel(q_ref, k_ref, v_ref, o_ref):
    q = q_ref[...]
    scale = HEAD_DIM_MEM ** -0.5
    for h in range(N_HEADS_MEM):
        sl = slice(h * HEAD_DIM_MEM, (h + 1) * HEAD_DIM_MEM)
        s = _bdot_nt(q[:, sl], k_ref[:, sl]) * scale
        s = s - jnp.max(s, axis=-1, keepdims=True)
        e = jnp.exp(s)
        pr = e / jnp.sum(e, axis=-1, keepdims=True)
        o_ref[:, sl] = _bdot(pr, v_ref[:, sl])


def _cross_attn(qm, row0, n_seq, seq_len, mem_k, mem_v, kv_index, *, tq, dst=None):
    nq = seq_len // tq
    rb0 = row0 // tq
    blk = tuple(None for _ in range(mem_k.ndim - 2)) + (N_MEM, D_MODEL)
    n_in = 3
    return pl.pallas_call(
        _cross_kernel if dst is None else _without_ref(_cross_kernel, n_in),
        grid=(n_seq, nq),
        input_output_aliases={} if dst is None else {n_in: 0},
        in_specs=[
            pl.BlockSpec((tq, D_MODEL), lambda n, i: (rb0 + n * nq + i, 0)),
            pl.BlockSpec(blk, lambda n, i: kv_index(n)),
            pl.BlockSpec(blk, lambda n, i: kv_index(n)),
        ] + ([] if dst is None else [_DST_SPEC]),
        out_specs=pl.BlockSpec((tq, D_MODEL), lambda n, i: (rb0 + n * nq + i, 0)),
        out_shape=jax.ShapeDtypeStruct(qm.shape, F32),
        compiler_params=_cparams("parallel", "arbitrary"),
        name="cross_attn",
    )(qm, mem_k, mem_v, *(() if dst is None else (dst,)))


def _oddeven_sort_pairs(n):
    pairs = []

    def merge(lo, hi, r):
        step = r * 2
        if step < hi - lo:
            merge(lo, hi, step)
            merge(lo + r, hi, step)
            pairs.extend((i, i + r) for i in range(lo + r, hi - r, step))
        else:
            pairs.append((lo, lo + r))

    def sort(lo, hi):
        if hi - lo >= 1:
            mid = lo + (hi - lo) // 2
            sort(lo, mid)
            sort(mid + 1, hi)
            merge(lo, hi, 1)

    sort(0, n - 1)
    return pairs


_SORT16 = _oddeven_sort_pairs(TOPK)


def _cmpx(xs, i, j):
    a, b = xs[i], xs[j]
    if a is None:
        xs[i], xs[j] = b, None
    elif b is not None:
        xs[i], xs[j] = jnp.maximum(a, b), jnp.minimum(a, b)


def _sort16_desc(xs):
    xs = list(xs)
    for i, j in _SORT16:
        _cmpx(xs, i, j)
    return xs


def _merge_top16(a, b):
    c = []
    for i in range(TOPK):
        x, y = a[i], b[TOPK - 1 - i]
        c.append(y if x is None else (x if y is None else jnp.maximum(x, y)))
    stride = TOPK // 2
    while stride:
        for i in range(TOPK):
            if not i & stride:
                _cmpx(c, i, i + stride)
        stride //= 2
    return c


def _top16_rows(st):
    lists = _sort16_desc([st[8 * v:8 * v + 8, :] for v in range(N_KEYS // 8)])
    for shift in (4, 2, 1):
        lists = _merge_top16(lists, [pltpu.roll(x, shift, 0) for x in lists])
    return lists


_CAND_PAIRS = [(i, j) for i in range(TOPK) for j in range(TOPK) if (i + 1) * (j + 1) <= TOPK]


def _peer_score_kernel(x_ref, g_ref, wq_ref, k1_ref, k2_ref,
                       xn_ref, s1_ref, p1_ref, s2_ref, p2_ref, tau_ref):
    h = pl.program_id(1)

    @pl.when(h == 0)
    def _():
        xn_ref[...] = _rms(x_ref[...], g_ref[...]).astype(BF16)

    q = jnp.dot(xn_ref[...], wq_ref[...].astype(BF16), preferred_element_type=F32)
    half = D_QUERY // 2
    s1 = _bdot_nt(k1_ref[...], q[:, :half])
    s2 = _bdot_nt(k2_ref[...], q[:, half:])
    v1 = _top16_rows(s1)
    v2 = _top16_rows(s2)
    cands = [v1[i] + v2[j] for i, j in _CAND_PAIRS]
    cands += [None] * (-len(cands) % TOPK)
    groups = [_sort16_desc(cands[g:g + TOPK]) for g in range(0, len(cands), TOPK)]
    top = groups[0]
    for grp in groups[1:]:
        top = _merge_top16(top, grp)
    z = jnp.exp(top[0] - top[0])
    for t in top[1:]:
        z = z + jnp.exp(t - top[0])
    s1_ref[...] = s1
    s2_ref[...] = s2
    p1_ref[...] = jnp.exp(s1 - v1[0][0:1, :]) / z[0:1, :]
    p2_ref[...] = jnp.exp(s2 - v2[0][0:1, :])
    tau_ref[...] = top[TOPK - 1]


def _peer_score(x, g, w_query, keys1, keys2, l, *, tm):
    t = x.shape[0]
    nh = N_RETR_HEADS
    key_spec = pl.BlockSpec((None, None, N_KEYS, D_QUERY // 2), lambda i, h: (l, h, 0, 0))
    st_spec = pl.BlockSpec((None, N_KEYS, tm), lambda i, h: (h, 0, i))
    st_shape = jax.ShapeDtypeStruct((nh, N_KEYS, t), F32)
    return pl.pallas_call(
        _peer_score_kernel,
        grid=(t // tm, nh),
        in_specs=[
            pl.BlockSpec((tm, D_MODEL), lambda i, h: (i, 0)),
            pl.BlockSpec((1, D_MODEL), lambda i, h: (0, 0)),
            pl.BlockSpec((None, D_MODEL, D_QUERY), lambda i, h: (l, 0, h)),
            key_spec, key_spec,
        ],
        out_specs=[
            pl.BlockSpec((tm, D_MODEL), lambda i, h: (i, 0)),
            st_spec, st_spec, st_spec, st_spec,
            pl.BlockSpec((None, 8, tm), lambda i, h: (h, 0, i)),
        ],
        out_shape=[jax.ShapeDtypeStruct((t, D_MODEL), BF16),
                   st_shape, st_shape, st_shape, st_shape,
                   jax.ShapeDtypeStruct((nh, 8, t), F32)],
        compiler_params=_cparams("parallel", "arbitrary"),
        name="peer_score",
    )(x, g.reshape(1, D_MODEL), w_query, keys1, keys2)


def _peer_dense_kernel(xn_ref, u_ref, v_ref, s1_ref, p1_ref, s2_ref, p2_ref, tau_ref, x_ref,
                       o_ref, *, n_a, a_blk):
    e = pl.program_id(1)

    @pl.when(e == 0)
    def _():
        o_ref[...] = x_ref[...]

    a0 = (e % (a_blk // n_a)) * n_a
    s1_rows = [[s1_ref[h, pl.ds(a0 + al, 1), :] for h in range(N_RETR_HEADS)] for al in range(n_a)]
    p1_rows = [[p1_ref[h, pl.ds(a0 + al, 1), :] for h in range(N_RETR_HEADS)] for al in range(n_a)]
    xn = xn_ref[...]
    a_per = MXU_DEPTH // N_KEYS
    groups = range(0, n_a, a_per)
    hts = [lax.dot_general(u_ref[g0 * N_KEYS:g0 * N_KEYS + MXU_DEPTH, :], xn, NT_DIMS,
                           preferred_element_type=F32) for g0 in groups]
    for ht, g0 in zip(hts, groups):
        blocks = []
        for al in range(g0, g0 + a_per):
            row = []
            for c in range(0, xn_ref.shape[0], N_KEYS):
                tok = slice(c, c + N_KEYS)
                g = None
                for h in range(N_RETR_HEADS):
                    s = s1_rows[al][h][:, tok] + s2_ref[h, :, tok]
                    w = p1_rows[al][h][:, tok] * p2_ref[h, :, tok]
                    hit = jnp.where(s >= tau_ref[h, 0:1, tok], w, 0.0)
                    g = hit if g is None else g + hit
                r0 = (al - g0) * N_KEYS
                row.append(g * _gelu(ht[r0:r0 + N_KEYS, tok]))
            blocks.append(jnp.concatenate(row, axis=1))
        wt = jnp.concatenate(blocks, axis=0)
        o_ref[...] += jnp.dot(wt.T.astype(BF16), v_ref[g0 * N_KEYS:g0 * N_KEYS + MXU_DEPTH, :],
                              preferred_element_type=F32)


def _peer_dense(xn, x_res, score, peer_u, peer_v, l, *, tm, te):
    s1t, p1t, s2t, p2t, tau = score
    t = xn.shape[0]
    n_exp = peer_u.shape[1]
    n_a = te // N_KEYS
    a_blk = max(n_a, 8)
    nh = N_RETR_HEADS
    assert te % MXU_DEPTH == 0
    once = pl.Buffered(1)
    return pl.pallas_call(
        functools.partial(_peer_dense_kernel, n_a=n_a, a_blk=a_blk),
        grid=(t // tm, n_exp // te),
        in_specs=[
            pl.BlockSpec((tm, D_MODEL), lambda i, e: (i, 0), pipeline_mode=once),
            pl.BlockSpec((None, te, D_MODEL), lambda i, e: (l, e, 0)),
            pl.BlockSpec((None, te, D_MODEL), lambda i, e: (l, e, 0)),
            pl.BlockSpec((nh, a_blk, tm), lambda i, e: (0, e * n_a // a_blk, i)),
            pl.BlockSpec((nh, a_blk, tm), lambda i, e: (0, e * n_a // a_blk, i)),
            pl.BlockSpec((nh, N_KEYS, tm), lambda i, e: (0, 0, i), pipeline_mode=once),
            pl.BlockSpec((nh, N_KEYS, tm), lambda i, e: (0, 0, i), pipeline_mode=once),
            pl.BlockSpec((nh, 8, tm), lambda i, e: (0, 0, i)),
            pl.BlockSpec((tm, D_MODEL), lambda i, e: (i, 0), pipeline_mode=once),
        ],
        out_specs=pl.BlockSpec((tm, D_MODEL), lambda i, e: (i, 0)),
        out_shape=jax.ShapeDtypeStruct((t, D_MODEL), F32),
        compiler_params=_cparams("parallel", "arbitrary"),
        name="peer_dense",
    )(xn, peer_u, peer_v, s1t, p1t, s2t, p2t, tau, x_res)


def kernel(x_prompt, x_sample, mem_prompt, cache_k, cache_v, cache_mem_k, cache_mem_v, state_ssm_re, state_ssm_im, state_pool, page_table, norm_mix, w_in, ssm_a_re, ssm_a_im, ssm_log_dt, ssm_b_re, ssm_b_im, ssm_c_re, ssm_c_im, ssm_d, ssm_w_glu, pool_w, pool_scale, att_lq1, att_lk1, att_lq2, att_lk2, att_subln, w_up_ssm, w_up_pool, w_up_att, w_out, norm_cross, norm_mem, w_mem_q, w_mem_k, w_mem_v, w_mem_o, norm_ffn, peer_w_query, peer_keys1, peer_keys2, peer_u, peer_v, norm_final):
    n_p, seq, d = x_prompt.shape
    n_s, dec_seq, _ = x_sample.shape
    depth = w_in.shape[0]
    tp = n_p * seq
    ts = n_s * dec_seq
    past_len = page_table.shape[1] * cache_k.shape[2]

    tm = 768
    tm_mm = 384
    tc_p, tc_s = 128, dec_seq

    x = jnp.concatenate([x_prompt.reshape(tp, d), x_sample.reshape(ts, d)], axis=0)
    mem_rows = mem_prompt.reshape(n_p * N_MEM, d)
    cmk = cache_mem_k.reshape(depth, n_s, N_MEM, d)
    cmv = cache_mem_v.reshape(depth, n_s, N_MEM, d)
    zeros_h = jnp.zeros((n_p, N_STATE), F32)
    zeros_buf = jnp.zeros((n_p, POOL_CARRY, W_POOL), F32)
    sample_buf = jnp.pad(state_pool, ((0, 0), (0, 0), (POOL_CARRY - POOL_BUF, 0), (0, 0)))
    (w_in, w_up_ssm, w_up_pool, w_up_att, w_out, w_mem_q, w_mem_k, w_mem_v, w_mem_o,
     peer_w_query, peer_u, peer_v) = (
        w.astype(BF16) for w in (w_in, w_up_ssm, w_up_pool, w_up_att, w_out, w_mem_q, w_mem_k,
                                 w_mem_v, w_mem_o, peer_w_query, peer_u, peer_v))

    outs = {k: [] for k in ("kp", "vp", "mkp", "mvp", "hrp", "hip", "bp", "ks", "vs", "hrs", "his", "bs")}
    for l in range(depth):
        a_mat = _in_proj(x, norm_mix[l], w_in, l, COL_GATE, tm=tm, tn=1024)
        k_all = a_mat[:, COL_K:COL_V]
        v_all = a_mat[:, COL_V:COL_GATE]
        outs["kp"].append(k_all[:tp].reshape(n_p, seq, N_HEADS_ATT, W_HEAD))
        outs["vp"].append(v_all[:tp].reshape(n_p, seq, N_HEADS_ATT, W_HEAD))
        outs["ks"].append(k_all[tp:].reshape(n_s, dec_seq, N_HEADS_ATT, W_HEAD))
        outs["vs"].append(v_all[tp:].reshape(n_s, dec_seq, N_HEADS_ATT, W_HEAD))

        s5_tab = _s5_tables(ssm_a_re[l], ssm_a_im[l], ssm_log_dt[l], ssm_b_re[l], ssm_b_im[l],
                            ssm_c_re[l], ssm_c_im[l])
        z, hrp, hip = _s5(a_mat, 0, n_p, seq, s5_tab, ssm_d[l],
                          zeros_h, zeros_h, ssm_w_glu, l, tc=tc_p)
        z, hrs, his = _s5(a_mat, tp, n_s, dec_seq, s5_tab, ssm_d[l],
                          state_ssm_re[l].reshape(n_s, N_STATE), state_ssm_im[l].reshape(n_s, N_STATE),
                          ssm_w_glu, l, tc=tc_s, dst=z)
        outs["hrp"].append(hrp); outs["hip"].append(hip)
        outs["hrs"].append(hrs); outs["his"].append(his)

        yb, bp = _pool(a_mat, 0, n_p, seq, zeros_buf, pool_w, pool_scale[l], l, 0, tc=tc_p)
        yb, bs = _pool(a_mat, tp, n_s, dec_seq, sample_buf[l], pool_w, pool_scale[l], l, past_len,
                       tc=tc_s, dst=yb)
        outs["bp"].append(bp); outs["bs"].append(bs)

        lam_init = 0.8 - 0.6 * math.exp(-0.3 * l)
        lam = (jnp.exp(jnp.sum(att_lq1[l] * att_lk1[l])) - jnp.exp(jnp.sum(att_lq2[l] * att_lk2[l])) + lam_init)
        lam_row = jnp.full((8, W_HEAD), lam, F32)
        yc = _attn_prompt(a_mat, n_p, seq, lam_row, att_subln[l], lam_init, tq=512, hb=2)
        yc = _attn_sample(a_mat, tp, n_s, dec_seq, cache_k, cache_v, page_table, lam_row,
                          att_subln[l], lam_init, l, yc, pages_per_step=8)

        merged = _gate_merge(x, norm_mix[l], z, yb, yc, w_in, w_up_ssm, w_up_pool, w_up_att, l,
                             tm=tm, tn=512)
        x = _matmul(merged, w_out, l, res=x, tm=tm_mm, name="w_out")

        mk_p = _matmul(mem_rows, w_mem_k, l, gain=norm_mem[l], tm=N_MEM, name="mem_k")
        mv_p = _matmul(mem_rows, w_mem_v, l, gain=norm_mem[l], tm=N_MEM, name="mem_v")
        outs["mkp"].append(mk_p.reshape(n_p, N_MEM, N_HEADS_MEM, HEAD_DIM_MEM))
        outs["mvp"].append(mv_p.reshape(n_p, N_MEM, N_HEADS_MEM, HEAD_DIM_MEM))
        qm = _matmul(x, w_mem_q, l, gain=norm_cross[l], tm=tm_mm, name="mem_q")
        ca = _cross_attn(qm, 0, n_p, seq, mk_p.reshape(n_p, N_MEM, d), mv_p.reshape(n_p, N_MEM, d),
                         lambda n: (n, 0, 0), tq=512)
        ca = _cross_attn(qm, tp, n_s, dec_seq, cmk, cmv, lambda n: (l, n, 0, 0), tq=dec_seq, dst=ca)
        x = _matmul(ca, w_mem_o, l, res=x, tm=tm_mm, name="mem_o")

        xn, *score = _peer_score(x, norm_ffn[l], peer_w_query, peer_keys1, peer_keys2, l, tm=tm)
        x = _peer_dense(xn, x, score, peer_u, peer_v, l, tm=tm, te=512)

    y = _final_norm(x, norm_final, tm=tm)
    st = lambda k: jnp.stack(outs[k])
    return (y[:tp].reshape(n_p, seq, d), y[tp:].reshape(n_s, dec_seq, d),
            st("kp"), st("vp"), st("mkp"), st("mvp"), st("hrp"), st("hip"), st("bp"),
            st("ks"), st("vs"), st("hrs"), st("his"), st("bs"))
```

```python
import functools
import math

import jax
import jax.numpy as jnp
from jax import lax
from jax.experimental import pallas as pl
from jax.experimental.pallas import tpu as pltpu

F32 = jnp.float32
BF16 = jnp.bfloat16
EPS = 1e-6
NEG_INF = -1e30
LOG2E = math.log2(math.e)

D_MODEL = 2048
W_SSM = 512
SSM_GROUP = 16
N_SSM_GROUPS = 32
SSM_STATE = 64
N_STATE = N_SSM_GROUPS * SSM_STATE
W_POOL = 512
POOL_WINDOWS = (2, 4, 8, 16)
POOL_GROUP = 128
POOL_BUF = 15
POOL_CARRY = 16
SCAN_GROUP = 8
N_HEADS_ATT = 8
HEAD_DIM_ATT = 64
W_HEAD = 2 * HEAD_DIM_ATT
W_ATT = N_HEADS_ATT * W_HEAD
N_MEM = 256
N_HEADS_MEM = 4
HEAD_DIM_MEM = 512
N_KEYS = 128
N_RETR_HEADS = 8
D_QUERY = 256
TOPK = 16

COL_SSM = 0
COL_POOL = W_SSM
COL_Q = W_SSM + W_POOL
COL_K = COL_Q + W_ATT
COL_V = COL_K + W_ATT
COL_GATE = COL_V + W_ATT
IN_COLS = COL_GATE + 3 * D_MODEL

VMEM_LIMIT_BYTES = 56 * 1024 * 1024
MXU_DEPTH = 256
PAGE_GROUP = 8
NT_DIMS = (((1,), (1,)), ((), ()))


def _cparams(*sem):
    return pltpu.CompilerParams(dimension_semantics=sem, vmem_limit_bytes=VMEM_LIMIT_BYTES)


def _bdot(a, b):
    return jnp.dot(a.astype(BF16), b.astype(BF16), preferred_element_type=F32)


def _bdot_nt(a, b):
    return lax.dot_general(a.astype(BF16), b.astype(BF16), NT_DIMS, preferred_element_type=F32)


def _rms(x, g):
    return x * lax.rsqrt(jnp.mean(x * x, axis=-1, keepdims=True) + EPS) * g


def _without_ref(fn, idx):
    def wrapped(*refs):
        return fn(*refs[:idx], *refs[idx + 1:])
    return wrapped


_DST_SPEC = pl.BlockSpec(memory_space=pl.ANY)


def _gelu(x):
    return 0.5 * x * (1.0 + lax.erf(x * (2.0 ** -0.5)))


def _in_proj_kernel(x_ref, g_ref, w_ref, o_ref, xn_ref):
    @pl.when(pl.program_id(1) == 0)
    def _():
        xn_ref[...] = _rms(x_ref[...], g_ref[...]).astype(BF16)

    o_ref[...] = jnp.dot(xn_ref[...], w_ref[...], preferred_element_type=F32)


def _in_proj(x, g, w_in, l, n_cols, *, tm, tn):
    t = x.shape[0]
    return pl.pallas_call(
        _in_proj_kernel,
        grid=(t // tm, n_cols // tn),
        in_specs=[
            pl.BlockSpec((tm, D_MODEL), lambda i, j: (i, 0)),
            pl.BlockSpec((1, D_MODEL), lambda i, j: (0, 0)),
            pl.BlockSpec((None, D_MODEL, tn), lambda i, j: (l, 0, j)),
        ],
        out_specs=pl.BlockSpec((tm, tn), lambda i, j: (i, j)),
        out_shape=jax.ShapeDtypeStruct((t, n_cols), F32),
        scratch_shapes=[pltpu.VMEM((tm, D_MODEL), BF16)],
        compiler_params=_cparams("parallel", "arbitrary"),
        name="in_proj",
    )(x, g.reshape(1, D_MODEL), w_in)


def _mm_kernel(*refs, has_norm, has_res):
    x_ref = refs[0]
    pos = 1
    g_ref = None
    if has_norm:
        g_ref = refs[pos]
        pos += 1
    w_ref = refs[pos]
    pos += 1
    r_ref = None
    if has_res:
        r_ref = refs[pos]
        pos += 1
    o_ref = refs[pos]
    x = x_ref[...]
    if has_norm:
        x = _rms(x, g_ref[...])
    res = jnp.dot(x.astype(BF16), w_ref[...], preferred_element_type=F32)
    if has_res:
        res = res + r_ref[...]
    o_ref[...] = res


def _matmul(x, w, l, *, gain=None, res=None, tm, name):
    t, k = x.shape
    n = w.shape[2]
    has_norm = gain is not None
    has_res = res is not None
    in_specs = [pl.BlockSpec((tm, k), lambda i: (i, 0))]
    args = [x]
    if has_norm:
        in_specs.append(pl.BlockSpec((1, k), lambda i: (0, 0)))
        args.append(gain.reshape(1, k))
    in_specs.append(pl.BlockSpec((None, k, n), lambda i: (l, 0, 0), pipeline_mode=pl.Buffered(1)))
    args.append(w)
    if has_res:
        in_specs.append(pl.BlockSpec((tm, n), lambda i: (i, 0)))
        args.append(res)
    return pl.pallas_call(
        functools.partial(_mm_kernel, has_norm=has_norm, has_res=has_res),
        grid=(t // tm,),
        in_specs=in_specs,
        out_specs=pl.BlockSpec((tm, n), lambda i: (i, 0)),
        out_shape=jax.ShapeDtypeStruct((t, n), F32),
        compiler_params=_cparams("parallel"),
        name=name,
    )(*args)


def _gate_merge_kernel(x_ref, g_ref, z_ref, yb_ref, yc_ref, wg0_ref, wg1_ref, wg2_ref,
                       wa_ref, wb_ref, wc_ref, o_ref, xn_ref):
    @pl.when(pl.program_id(1) == 0)
    def _():
        xn_ref[...] = _rms(x_ref[...], g_ref[...]).astype(BF16)

    xn = xn_ref[...]
    acc = None
    for y_ref, wg_ref, wu_ref in ((z_ref, wg0_ref, wa_ref), (yb_ref, wg1_ref, wb_ref),
                                  (yc_ref, wg2_ref, wc_ref)):
        gate = jax.nn.sigmoid(jnp.dot(xn, wg_ref[...], preferred_element_type=F32))
        term = gate * _bdot(y_ref[...], wu_ref[...])
        acc = term if acc is None else acc + term
    o_ref[...] = acc


def _gate_merge(x, g, z, yb, yc, w_in, w_up_ssm, w_up_pool, w_up_att, l, *, tm, tn):
    t = x.shape[0]
    nb = D_MODEL // tn
    gb0 = COL_GATE // tn
    gate_spec = lambda b: pl.BlockSpec((None, D_MODEL, tn), lambda i, j: (l, 0, gb0 + b * nb + j))
    return pl.pallas_call(
        _gate_merge_kernel,
        grid=(t // tm, nb),
        in_specs=[
            pl.BlockSpec((tm, D_MODEL), lambda i, j: (i, 0)),
            pl.BlockSpec((1, D_MODEL), lambda i, j: (0, 0)),
            pl.BlockSpec((tm, W_SSM), lambda i, j: (i, 0)),
            pl.BlockSpec((tm, W_POOL), lambda i, j: (i, 0)),
            pl.BlockSpec((tm, W_ATT), lambda i, j: (i, 0)),
            gate_spec(0), gate_spec(1), gate_spec(2),
            pl.BlockSpec((None, W_SSM, tn), lambda i, j: (l, 0, j)),
            pl.BlockSpec((None, W_POOL, tn), lambda i, j: (l, 0, j)),
            pl.BlockSpec((None, W_ATT, tn), lambda i, j: (l, 0, j)),
        ],
        out_specs=pl.BlockSpec((tm, tn), lambda i, j: (i, j)),
        out_shape=jax.ShapeDtypeStruct((t, D_MODEL), F32),
        scratch_shapes=[pltpu.VMEM((tm, D_MODEL), BF16)],
        compiler_params=_cparams("parallel", "arbitrary"),
        name="gate_merge",
    )(x, g.reshape(1, D_MODEL), z, yb, yc, w_in, w_in, w_in, w_up_ssm, w_up_pool, w_up_att)


def _final_norm_kernel(x_ref, g_ref, yp_ref, ys_ref, *, n_prompt_tiles):
    y = _rms(x_ref[...], g_ref[...])

    @pl.when(pl.program_id(0) < n_prompt_tiles)
    def _():
        yp_ref[...] = y

    @pl.when(pl.program_id(0) >= n_prompt_tiles)
    def _():
        ys_ref[...] = y


def _final_norm(x, g, tp, *, tm):
    t = x.shape[0]
    assert tp % tm == 0 and (t - tp) % tm == 0
    npt = tp // tm
    return pl.pallas_call(
        functools.partial(_final_norm_kernel, n_prompt_tiles=npt),
        grid=(t // tm,),
        in_specs=[pl.BlockSpec((tm, D_MODEL), lambda i: (i, 0)),
                  pl.BlockSpec((1, D_MODEL), lambda i: (0, 0))],
        out_specs=[pl.BlockSpec((tm, D_MODEL), lambda i: (jnp.minimum(i, npt - 1), 0)),
                   pl.BlockSpec((tm, D_MODEL), lambda i: (jnp.maximum(i - npt, 0), 0))],
        out_shape=[jax.ShapeDtypeStruct((tp, D_MODEL), F32),
                   jax.ShapeDtypeStruct((t - tp, D_MODEL), F32)],
        compiler_params=_cparams("arbitrary"),
        name="final_norm",
    )(x, g.reshape(1, D_MODEL))


def _s5_tables(a_re, a_im, log_dt, b_re, b_im, c_re, c_im):
    g_, p_ = N_SSM_GROUPS, SSM_STATE
    dt = jnp.exp(log_dt)[:, None]
    mag = jnp.exp(a_re * dt)
    lb_re = mag * jnp.cos(a_im * dt)
    lb_im = mag * jnp.sin(a_im * dt)
    den = a_re * a_re + a_im * a_im
    nr = lb_re - 1.0
    f_re = (nr * a_re + lb_im * a_im) / den
    f_im = (lb_im * a_re - nr * a_im) / den
    bb_re = f_re[..., None] * b_re - f_im[..., None] * b_im
    bb_im = f_re[..., None] * b_im + f_im[..., None] * b_re
    eye = jnp.eye(g_, dtype=F32)
    bd_re = jnp.einsum('gpc,gh->gchp', bb_re, eye).reshape(W_SSM, N_STATE)
    bd_im = jnp.einsum('gpc,gh->gchp', bb_im, eye).reshape(W_SSM, N_STATE)
    bmat = jnp.concatenate([bd_re, bd_im], axis=1).astype(BF16)
    cd_re = jnp.einsum('gcp,gh->gphc', c_re, eye).reshape(N_STATE, W_SSM).astype(BF16)
    cd_im = jnp.einsum('gcp,gh->gphc', -c_im, eye).reshape(N_STATE, W_SSM).astype(BF16)
    lr = lb_re.reshape(1, N_STATE)
    li = lb_im.reshape(1, N_STATE)
    nsteps = int(math.log2(SCAN_GROUP))
    sq_re, sq_im = [lr], [li]
    for _ in range(nsteps):
        r, i = sq_re[-1], sq_im[-1]
        sq_re.append(r * r - i * i)
        sq_im.append(2.0 * r * i)
    pt_re, pt_im = lr, li
    for s in range(nsteps):
        r, i = sq_re[s], sq_im[s]
        pt_re, pt_im = (jnp.concatenate([pt_re, pt_re * r - pt_im * i], axis=0),
                        jnp.concatenate([pt_im, pt_re * i + pt_im * r], axis=0))
    pw_re = jnp.concatenate(sq_re[:nsteps], axis=0)
    pw_im = jnp.concatenate(sq_im[:nsteps], axis=0)
    return bmat, cd_re, cd_im, pw_re, pw_im, pt_re, pt_im


def _s5_kernel(u_ref, b_ref, cre_ref, cim_ref, d_ref, pwr_ref, pwi_ref, ptr_ref, pti_ref,
               h0r_ref, h0i_ref, wglu_ref, z_ref, hr_ref, hi_ref, cr_ref, ci_ref, *, tc, nsteps):
    c = pl.program_id(1)

    @pl.when(c == 0)
    def _():
        cr_ref[...] = h0r_ref[...]
        ci_ref[...] = h0i_ref[...]

    u = u_ref[...]
    bu = jnp.dot(u.astype(BF16), b_ref[...], preferred_element_type=F32)
    re = bu[:, :N_STATE]
    im = bu[:, N_STATE:]
    row = lax.broadcasted_iota(jnp.int32, (tc, 1), 0) % SCAN_GROUP
    for s in range(nsteps):
        k = 1 << s
        keep = row >= k
        sre = jnp.where(keep, pltpu.roll(re, k, 0), 0.0)
        sim = jnp.where(keep, pltpu.roll(im, k, 0), 0.0)
        ar = pwr_ref[s:s + 1, :]
        ai = pwi_ref[s:s + 1, :]
        re, im = re + (ar * sre - ai * sim), im + (ar * sim + ai * sre)
    hr = cr_ref[...]
    hi = ci_ref[...]
    pr = ptr_ref[...]
    pi = pti_ref[...]
    res, ims = [], []
    for j in range(0, tc, SCAN_GROUP):
        gr = re[j:j + SCAN_GROUP] + (pr * hr - pi * hi)
        gi = im[j:j + SCAN_GROUP] + (pr * hi + pi * hr)
        hr = gr[SCAN_GROUP - 1:SCAN_GROUP, :]
        hi = gi[SCAN_GROUP - 1:SCAN_GROUP, :]
        res.append(gr)
        ims.append(gi)
    re = jnp.concatenate(res, axis=0)
    im = jnp.concatenate(ims, axis=0)
    cr_ref[...] = hr
    ci_ref[...] = hi
    y = (jnp.dot(re.astype(BF16), cre_ref[...], preferred_element_type=F32)
         + jnp.dot(im.astype(BF16), cim_ref[...], preferred_element_type=F32)
         + d_ref[...] * u)
    z = _gelu(y)
    z_ref[...] = z * jax.nn.sigmoid(_bdot(z, wglu_ref[...]))

    @pl.when(c == pl.num_programs(1) - 1)
    def _():
        hr_ref[...] = re[tc - 1:tc, :]
        hi_ref[...] = im[tc - 1:tc, :]


def _s5(a_mat, row0, n_seq, seq_len, tables, d_skip, h0_re, h0_im, w_glu, l, *, tc, dst=None):
    bmat, cd_re, cd_im, pw_re, pw_im, pt_re, pt_im = tables
    nsteps = pw_re.shape[0]
    assert tc % SCAN_GROUP == 0
    nc = seq_len // tc
    rb0 = row0 // tc
    const = lambda shape: pl.BlockSpec(shape, lambda n, c: tuple(0 for _ in shape))
    kern = functools.partial(_s5_kernel, tc=tc, nsteps=nsteps)
    n_in = 12
    z, hr, hi = pl.pallas_call(
        kern if dst is None else _without_ref(kern, n_in),
        grid=(n_seq, nc),
        input_output_aliases={} if dst is None else {n_in: 0},
        in_specs=[
            pl.BlockSpec((tc, W_SSM), lambda n, c: (rb0 + n * nc + c, COL_SSM // W_SSM)),
            const((W_SSM, 2 * N_STATE)),
            const((N_STATE, W_SSM)),
            const((N_STATE, W_SSM)),
            const((1, W_SSM)),
            const(pw_re.shape),
            const(pw_im.shape),
            const((SCAN_GROUP, N_STATE)),
            const((SCAN_GROUP, N_STATE)),
            pl.BlockSpec((None, 1, N_STATE), lambda n, c: (n, 0, 0)),
            pl.BlockSpec((None, 1, N_STATE), lambda n, c: (n, 0, 0)),
            pl.BlockSpec((None, W_SSM, W_SSM), lambda n, c: (l, 0, 0)),
        ] + ([] if dst is None else [_DST_SPEC]),
        out_specs=[
            pl.BlockSpec((tc, W_SSM), lambda n, c: (rb0 + n * nc + c, 0)),
            pl.BlockSpec((None, 1, N_STATE), lambda n, c: (n, 0, 0)),
            pl.BlockSpec((None, 1, N_STATE), lambda n, c: (n, 0, 0)),
        ],
        out_shape=[jax.ShapeDtypeStruct((a_mat.shape[0], W_SSM), F32),
                   jax.ShapeDtypeStruct((n_seq, 1, N_STATE), F32),
                   jax.ShapeDtypeStruct((n_seq, 1, N_STATE), F32)],
        scratch_shapes=[pltpu.VMEM((1, N_STATE), F32), pltpu.VMEM((1, N_STATE), F32)],
        compiler_params=_cparams("parallel", "arbitrary"),
        name="s5",
    )(a_mat, bmat, cd_re, cd_im, d_skip.reshape(1, W_SSM), pw_re, pw_im, pt_re, pt_im,
      h0_re.reshape(n_seq, 1, N_STATE), h0_im.reshape(n_seq, 1, N_STATE), w_glu,
      *(() if dst is None else (dst,)))
    shp = (n_seq, N_SSM_GROUPS, SSM_STATE)
    return z, hr.reshape(shp), hi.reshape(shp)


def _pool_kernel(u_ref, buf_ref, w_ref, sc_ref, y_ref, nb_ref, xx_ref, *, tc, start_pos):
    c = pl.program_id(1)

    @pl.when(c == 0)
    def _():
        xx_ref[0:POOL_CARRY, :] = buf_ref[...]

    u = u_ref[...]
    xx_ref[POOL_CARRY:, :] = u
    x = xx_ref[...]
    sums = []
    s = x
    for w in (1, 2, 4, 8):
        s = s + pltpu.roll(s, w, 0)
        sums.append(s)
    t_abs = (start_pos + c * tc + lax.broadcasted_iota(jnp.int32, (tc, 1), 0) + 1).astype(F32)
    outs = []
    for gi, w in enumerate(POOL_WINDOWS):
        sl = slice(gi * POOL_GROUP, (gi + 1) * POOL_GROUP)
        mean = sums[gi][POOL_CARRY:, sl] / jnp.minimum(t_abs, float(w))
        d = mean - u[:, sl]
        outs.append(_bdot(d, w_ref[gi]))
    y_ref[...] = jnp.concatenate(outs, axis=-1) * sc_ref[...]
    carry = x[tc:, :]
    xx_ref[0:POOL_CARRY, :] = carry

    @pl.when(c == pl.num_programs(1) - 1)
    def _():
        nb_ref[...] = carry


def _pool(a_mat, row0, n_seq, seq_len, buf16, pool_w, pool_scale, l, start_pos, *, tc, dst=None):
    nc = seq_len // tc
    rb0 = row0 // tc
    kern = functools.partial(_pool_kernel, tc=tc, start_pos=start_pos)
    n_in = 4
    y, nb = pl.pallas_call(
        kern if dst is None else _without_ref(kern, n_in),
        grid=(n_seq, nc),
        input_output_aliases={} if dst is None else {n_in: 0},
        in_specs=[
            pl.BlockSpec((tc, W_POOL), lambda n, c: (rb0 + n * nc + c, COL_POOL // W_POOL)),
            pl.BlockSpec((None, POOL_CARRY, W_POOL), lambda n, c: (n, 0, 0)),
            pl.BlockSpec((None, len(POOL_WINDOWS), POOL_GROUP, POOL_GROUP), lambda n, c: (l, 0, 0, 0)),
            pl.BlockSpec((1, W_POOL), lambda n, c: (0, 0)),
        ] + ([] if dst is None else [_DST_SPEC]),
        out_specs=[
            pl.BlockSpec((tc, W_POOL), lambda n, c: (rb0 + n * nc + c, 0)),
            pl.BlockSpec((None, POOL_CARRY, W_POOL), lambda n, c: (n, 0, 0)),
        ],
        out_shape=[jax.ShapeDtypeStruct((a_mat.shape[0], W_POOL), F32),
                   jax.ShapeDtypeStruct((n_seq, POOL_CARRY, W_POOL), F32)],
        scratch_shapes=[pltpu.VMEM((POOL_CARRY + tc, W_POOL), F32)],
        compiler_params=_cparams("parallel", "arbitrary"),
        name="pool",
    )(a_mat, buf16, pool_w, pool_scale.reshape(1, W_POOL), *(() if dst is None else (dst,)))
    return y, nb[:, POOL_CARRY - POOL_BUF:, :]


def _subln(o, g, lam_init):
    return _rms(o, g) * (1.0 - lam_init)


def _attn_prompt_kernel(qt_ref, kt_ref, q_ref, k_ref, v_ref, lam_ref, g_ref, o_ref,
                        m_ref, l_ref, acc_ref, *, tq, hb, lam_init):
    p = pl.program_id(2)
    qi = qt_ref[p]
    ki = kt_ref[p]

    @pl.when(ki == 0)
    def _():
        m_ref[...] = jnp.full(m_ref.shape, NEG_INF, F32)
        l_ref[...] = jnp.zeros(l_ref.shape, F32)
        acc_ref[...] = jnp.zeros(acc_ref.shape, F32)

    def accumulate(masked):
        q = (q_ref[...] * (HEAD_DIM_ATT ** -0.5 * LOG2E)).astype(BF16)
        k = k_ref[...].astype(BF16)
        v = v_ref[...].astype(BF16)
        if masked:
            causal = (lax.broadcasted_iota(jnp.int32, (tq, tq), 0)
                      >= lax.broadcasted_iota(jnp.int32, (tq, tq), 1))
        for hh in range(hb):
            vh = v[:, hh * W_HEAD:(hh + 1) * W_HEAD]
            for c in range(2):
                idx = 2 * hh + c
                sl = slice(hh * W_HEAD + c * HEAD_DIM_ATT, hh * W_HEAD + (c + 1) * HEAD_DIM_ATT)
                s = lax.dot_general(q[:, sl], k[:, sl], NT_DIMS, preferred_element_type=F32)
                if masked:
                    s = jnp.where(causal, s, NEG_INF)
                m_prev = m_ref[idx]
                m_new = jnp.maximum(m_prev, jnp.max(s, axis=-1, keepdims=True))
                alpha = jnp.exp2(m_prev - m_new)
                pexp = jnp.exp2(s - jnp.concatenate([m_new] * (tq // W_HEAD), axis=1))
                l_ref[idx] = alpha * l_ref[idx] + jnp.sum(pexp, axis=-1, keepdims=True)
                acc_ref[idx] = alpha * acc_ref[idx] + jnp.dot(pexp.astype(BF16), vh,
                                                              preferred_element_type=F32)
                m_ref[idx] = m_new

    @pl.when(ki < qi)
    def _():
        accumulate(False)

    @pl.when(ki == qi)
    def _():
        accumulate(True)
        lam = lam_ref[0:1, :]
        for hh in range(hb):
            o = (acc_ref[2 * hh] / l_ref[2 * hh]
                 - lam * (acc_ref[2 * hh + 1] / l_ref[2 * hh + 1]))
            o_ref[:, hh * W_HEAD:(hh + 1) * W_HEAD] = _subln(o, g_ref[...], lam_init)


def _attn_prompt(a_mat, n_seq, seq_len, lam_row, subln_g, lam_init, *, tq, hb):
    nq = seq_len // tq
    pairs = [(qi, ki) for qi in range(nq) for ki in range(qi + 1)]
    qt = jnp.asarray([p[0] for p in pairs], jnp.int32)
    kt = jnp.asarray([p[1] for p in pairs], jnp.int32)
    wb = hb * W_HEAD
    cq, ck, cv = COL_Q // wb, COL_K // wb, COL_V // wb
    grid_spec = pltpu.PrefetchScalarGridSpec(
        num_scalar_prefetch=2,
        grid=(n_seq, N_HEADS_ATT // hb, len(pairs)),
        in_specs=[
            pl.BlockSpec((tq, wb), lambda b, h, p, qt, kt: (b * nq + qt[p], cq + h)),
            pl.BlockSpec((tq, wb), lambda b, h, p, qt, kt: (b * nq + kt[p], ck + h)),
            pl.BlockSpec((tq, wb), lambda b, h, p, qt, kt: (b * nq + kt[p], cv + h)),
            pl.BlockSpec((8, W_HEAD), lambda b, h, p, qt, kt: (0, 0)),
            pl.BlockSpec((1, W_HEAD), lambda b, h, p, qt, kt: (0, 0)),
        ],
        out_specs=pl.BlockSpec((tq, wb), lambda b, h, p, qt, kt: (b * nq + qt[p], h)),
        scratch_shapes=[pltpu.VMEM((2 * hb, tq, W_HEAD), F32), pltpu.VMEM((2 * hb, tq, W_HEAD), F32),
                        pltpu.VMEM((2 * hb, tq, W_HEAD), F32)],
    )
    return pl.pallas_call(
        functools.partial(_attn_prompt_kernel, tq=tq, hb=hb, lam_init=lam_init),
        grid_spec=grid_spec,
        out_shape=jax.ShapeDtypeStruct((a_mat.shape[0], W_ATT), F32),
        compiler_params=_cparams("parallel", "parallel", "arbitrary"),
        name="attn_prompt",
    )(qt, kt, a_mat, a_mat, a_mat, lam_row, subln_g.reshape(1, W_HEAD))


def _attn_sample_kernel(pt_ref, q_ref, kn_ref, vn_ref, lam_ref, g_ref, *rest, n_pages, dec_seq, lam_init):
    k_refs = rest[:n_pages]
    v_refs = rest[n_pages:2 * n_pages]
    o_ref, qh_ref, bias_ref, m_ref, l_ref, acc_ref = rest[2 * n_pages:]
    step = pl.program_id(1)
    rph = 2 * dec_seq

    @pl.when(step == 0)
    def _():
        q = q_ref[...] * (HEAD_DIM_ATT ** -0.5)
        row = lax.broadcasted_iota(jnp.int32, (rph, W_HEAD), 0)
        lane = lax.broadcasted_iota(jnp.int32, (rph, W_HEAD), 1)
        keep = lane // HEAD_DIM_ATT == row // dec_seq
        for h in range(N_HEADS_ATT):
            qh = q[:, h * W_HEAD:(h + 1) * W_HEAD]
            qh_ref[h] = jnp.where(keep, jnp.concatenate([qh, qh], axis=0), 0.0).astype(BF16)
        m_ref[...] = jnp.full(m_ref.shape, NEG_INF, F32)
        l_ref[...] = jnp.zeros(l_ref.shape, F32)
        acc_ref[...] = jnp.zeros(acc_ref.shape, F32)
        brow = lax.broadcasted_iota(jnp.int32, bias_ref.shape, 0) // rph
        bcol = lax.broadcasted_iota(jnp.int32, bias_ref.shape, 1) % N_HEADS_ATT
        bias_ref[...] = jnp.where(brow == bcol, 0.0, NEG_INF)

    def update(s, pv_of):
        m_prev = m_ref[...]
        m_new = jnp.maximum(m_prev, jnp.max(s, axis=-1, keepdims=True))
        alpha = jnp.exp(m_prev - m_new)
        pexp = jnp.exp(s - m_new)
        l_ref[...] = alpha * l_ref[...] + jnp.sum(pexp, axis=-1, keepdims=True)
        acc_ref[...] = alpha * acc_ref[...] + pv_of(pexp.astype(BF16))
        m_ref[...] = m_new

    q_all = jnp.concatenate([qh_ref[h] for h in range(N_HEADS_ATT)], axis=0)
    for g0 in range(0, n_pages, PAGE_GROUP):
        k_all = jnp.concatenate([r[...].astype(BF16) for r in k_refs[g0:g0 + PAGE_GROUP]], axis=0)
        v_all = jnp.concatenate([r[...].astype(BF16) for r in v_refs[g0:g0 + PAGE_GROUP]], axis=0)
        s = lax.dot_general(q_all, k_all, NT_DIMS, preferred_element_type=F32)
        s = s + jnp.concatenate([bias_ref[...]] * PAGE_GROUP, axis=1)
        update(s, lambda pb, v_all=v_all: jnp.dot(pb, v_all, preferred_element_type=F32))

    @pl.when(step == pl.num_programs(1) - 1)
    def _():
        pad = jnp.zeros((W_HEAD - dec_seq, W_HEAD), F32)

        def new_rows(ref, h):
            return jnp.concatenate([ref[:, h * W_HEAD:(h + 1) * W_HEAD], pad], axis=0).astype(BF16)

        s2 = jnp.concatenate(
            [lax.dot_general(qh_ref[h], new_rows(kn_ref, h), NT_DIMS, preferred_element_type=F32)
             for h in range(N_HEADS_ATT)], axis=0)
        qi = lax.broadcasted_iota(jnp.int32, s2.shape, 0) % dec_seq
        kj = lax.broadcasted_iota(jnp.int32, s2.shape, 1)
        s2 = jnp.where(qi >= kj, s2, NEG_INF)
        update(s2, lambda pb: jnp.concatenate(
            [jnp.dot(pb[h * rph:(h + 1) * rph], new_rows(vn_ref, h), preferred_element_type=F32)
             for h in range(N_HEADS_ATT)], axis=0))
        o = acc_ref[...] / l_ref[...]
        lam = lam_ref[0:1, :]
        for h in range(N_HEADS_ATT):
            r0 = h * rph
            oh = o[r0:r0 + dec_seq] - lam * o[r0 + dec_seq:r0 + rph]
            o_ref[:, h * W_HEAD:(h + 1) * W_HEAD] = _subln(oh, g_ref[...], lam_init)


def _attn_sample(a_mat, row0, n_seq, dec_seq, cache_k, cache_v, page_table, lam_row, subln_g,
                 lam_init, l, dst, *, pages_per_step):
    depth, n_pool, page, _, _ = cache_k.shape
    ck = cache_k.reshape(depth, n_pool, page * N_HEADS_ATT, W_HEAD)
    cv = cache_v.reshape(depth, n_pool, page * N_HEADS_ATT, W_HEAD)
    n_pt = page_table.shape[1]
    r = pages_per_step
    rb0 = row0 // dec_seq
    cq, ckk, cvv = COL_Q // W_ATT, COL_K // W_ATT, COL_V // W_ATT

    def page_spec(j):
        return pl.BlockSpec((None, None, page * N_HEADS_ATT, W_HEAD),
                            lambda n, s, pt: (l, pt[n, s * r + j], 0, 0))

    grid_spec = pltpu.PrefetchScalarGridSpec(
        num_scalar_prefetch=1,
        grid=(n_seq, n_pt // r),
        in_specs=[
            pl.BlockSpec((dec_seq, W_ATT), lambda n, s, pt: (rb0 + n, cq)),
            pl.BlockSpec((dec_seq, W_ATT), lambda n, s, pt: (rb0 + n, ckk)),
            pl.BlockSpec((dec_seq, W_ATT), lambda n, s, pt: (rb0 + n, cvv)),
            pl.BlockSpec((8, W_HEAD), lambda n, s, pt: (0, 0)),
            pl.BlockSpec((1, W_HEAD), lambda n, s, pt: (0, 0)),
        ] + [page_spec(j) for j in range(r)] + [page_spec(j) for j in range(r)] + [_DST_SPEC],
        out_specs=pl.BlockSpec((dec_seq, W_ATT), lambda n, s, pt: (rb0 + n, 0)),
        scratch_shapes=[pltpu.VMEM((N_HEADS_ATT, 2 * dec_seq, W_HEAD), BF16),
                        pltpu.VMEM((2 * N_HEADS_ATT * dec_seq, page * N_HEADS_ATT), F32),
                        pltpu.VMEM((2 * N_HEADS_ATT * dec_seq, 1), F32),
                        pltpu.VMEM((2 * N_HEADS_ATT * dec_seq, 1), F32),
                        pltpu.VMEM((2 * N_HEADS_ATT * dec_seq, W_HEAD), F32)],
    )
    n_in = 5 + 2 * r
    kern = functools.partial(_attn_sample_kernel, n_pages=r, dec_seq=dec_seq, lam_init=lam_init)
    return pl.pallas_call(
        _without_ref(kern, 1 + n_in),
        grid_spec=grid_spec,
        input_output_aliases={1 + n_in: 0},
        out_shape=jax.ShapeDtypeStruct(dst.shape, F32),
        compiler_params=_cparams("parallel", "arbitrary"),
        name="attn_sample",
    )(page_table, a_mat, a_mat, a_mat, lam_row, subln_g.reshape(1, W_HEAD),
      *([ck] * r), *([cv] * r), dst)


def _cross_kernel(q_ref, k_ref, v_ref, o_ref):
    q = q_ref[...]
    scale = HEAD_DIM_MEM ** -0.5
    for h in range(N_HEADS_MEM):
        sl = slice(h * HEAD_DIM_MEM, (h + 1) * HEAD_DIM_MEM)
        s = _bdot_nt(q[:, sl], k_ref[:, sl]) * scale
        s = s - jnp.max(s, axis=-1, keepdims=True)
        e = jnp.exp(s)
        pr = e / jnp.sum(e, axis=-1, keepdims=True)
        o_ref[:, sl] = _bdot(pr, v_ref[:, sl])


def _cross_attn(qm, row0, n_seq, seq_len, mem_k, mem_v, kv_index, *, tq, dst=None):
    nq = seq_len // tq
    rb0 = row0 // tq
    blk = tuple(None for _ in range(mem_k.ndim - 2)) + (N_MEM, D_MODEL)
    n_in = 3
    return pl.pallas_call(
        _cross_kernel if dst is None else _without_ref(_cross_kernel, n_in),
        grid=(n_seq, nq),
        input_output_aliases={} if dst is None else {n_in: 0},
        in_specs=[
            pl.BlockSpec((tq, D_MODEL), lambda n, i: (rb0 + n * nq + i, 0)),
            pl.BlockSpec(blk, lambda n, i: kv_index(n)),
            pl.BlockSpec(blk, lambda n, i: kv_index(n)),
        ] + ([] if dst is None else [_DST_SPEC]),
        out_specs=pl.BlockSpec((tq, D_MODEL), lambda n, i: (rb0 + n * nq + i, 0)),
        out_shape=jax.ShapeDtypeStruct(qm.shape, F32),
        compiler_params=_cparams("parallel", "arbitrary"),
        name="cross_attn",
    )(qm, mem_k, mem_v, *(() if dst is None else (dst,)))


def _oddeven_sort_pairs(n):
    pairs = []

    def merge(lo, hi, r):
        step = r * 2
        if step < hi - lo:
            merge(lo, hi, step)
            merge(lo + r, hi, step)
            pairs.extend((i, i + r) for i in range(lo + r, hi - r, step))
        else:
            pairs.append((lo, lo + r))

    def sort(lo, hi):
        if hi - lo >= 1:
            mid = lo + (hi - lo) // 2
            sort(lo, mid)
            sort(mid + 1, hi)
            merge(lo, hi, 1)

    sort(0, n - 1)
    return pairs


_SORT16 = _oddeven_sort_pairs(TOPK)


def _cmpx(xs, i, j):
    a, b = xs[i], xs[j]
    if a is None:
        xs[i], xs[j] = b, None
    elif b is not None:
        xs[i], xs[j] = jnp.maximum(a, b), jnp.minimum(a, b)


def _sort16_desc(xs):
    xs = list(xs)
    for i, j in _SORT16:
        _cmpx(xs, i, j)
    return xs


def _merge_top16(a, b):
    c = []
    for i in range(TOPK):
        x, y = a[i], b[TOPK - 1 - i]
        c.append(y if x is None else (x if y is None else jnp.maximum(x, y)))
    stride = TOPK // 2
    while stride:
        for i in range(TOPK):
            if not i & stride:
                _cmpx(c, i, i + stride)
        stride //= 2
    return c


def _top16_rows(st):
    lists = _sort16_desc([st[8 * v:8 * v + 8, :] for v in range(N_KEYS // 8)])
    for shift in (4, 2, 1):
        lists = _merge_top16(lists, [pltpu.roll(x, shift, 0) for x in lists])
    return lists


_CAND_PAIRS = [(i, j) for i in range(TOPK) for j in range(TOPK) if (i + 1) * (j + 1) <= TOPK]


def _peer_score_kernel(x_ref, g_ref, wq_ref, k1_ref, k2_ref,
                       xn_ref, s1_ref, p1_ref, s2_ref, p2_ref, tau_ref):
    h = pl.program_id(1)

    @pl.when(h == 0)
    def _():
        xn_ref[...] = _rms(x_ref[...], g_ref[...]).astype(BF16)

    q = jnp.dot(xn_ref[...], wq_ref[...].astype(BF16), preferred_element_type=F32)
    half = D_QUERY // 2
    s1 = _bdot_nt(k1_ref[...], q[:, :half])
    s2 = _bdot_nt(k2_ref[...], q[:, half:])
    v1 = _top16_rows(s1)
    v2 = _top16_rows(s2)
    cands = [v1[i] + v2[j] for i, j in _CAND_PAIRS]
    cands += [None] * (-len(cands) % TOPK)
    groups = [_sort16_desc(cands[g:g + TOPK]) for g in range(0, len(cands), TOPK)]
    top = groups[0]
    for grp in groups[1:]:
        top = _merge_top16(top, grp)
    z = jnp.exp(top[0] - top[0])
    for t in top[1:]:
        z = z + jnp.exp(t - top[0])
    s1_ref[...] = s1
    s2_ref[...] = s2
    p1_ref[...] = jnp.exp(s1 - v1[0][0:1, :]) / z[0:1, :]
    p2_ref[...] = jnp.exp(s2 - v2[0][0:1, :])
    tau_ref[...] = top[TOPK - 1]


def _peer_score(x, g, w_query, keys1, keys2, l, *, tm):
    t = x.shape[0]
    nh = N_RETR_HEADS
    key_spec = pl.BlockSpec((None, None, N_KEYS, D_QUERY // 2), lambda i, h: (l, h, 0, 0))
    st_spec = pl.BlockSpec((None, N_KEYS, tm), lambda i, h: (h, 0, i))
    st_shape = jax.ShapeDtypeStruct((nh, N_KEYS, t), F32)
    return pl.pallas_call(
        _peer_score_kernel,
        grid=(t // tm, nh),
        in_specs=[
            pl.BlockSpec((tm, D_MODEL), lambda i, h: (i, 0)),
            pl.BlockSpec((1, D_MODEL), lambda i, h: (0, 0)),
            pl.BlockSpec((None, D_MODEL, D_QUERY), lambda i, h: (l, 0, h)),
            key_spec, key_spec,
        ],
        out_specs=[
            pl.BlockSpec((tm, D_MODEL), lambda i, h: (i, 0)),
            st_spec, st_spec, st_spec, st_spec,
            pl.BlockSpec((None, 8, tm), lambda i, h: (h, 0, i)),
        ],
        out_shape=[jax.ShapeDtypeStruct((t, D_MODEL), BF16),
                   st_shape, st_shape, st_shape, st_shape,
                   jax.ShapeDtypeStruct((nh, 8, t), F32)],
        compiler_params=_cparams("parallel", "arbitrary"),
        name="peer_score",
    )(x, g.reshape(1, D_MODEL), w_query, keys1, keys2)


def _peer_dense_kernel(xn_ref, u_ref, v_ref, s1_ref, p1_ref, s2_ref, p2_ref, tau_ref, x_ref,
                       o_ref, *, n_a, a_blk):
    e = pl.program_id(1)

    @pl.when(e == 0)
    def _():
        o_ref[...] = x_ref[...]

    a0 = (e % (a_blk // n_a)) * n_a
    s1_rows = [[s1_ref[h, pl.ds(a0 + al, 1), :] for h in range(N_RETR_HEADS)] for al in range(n_a)]
    p1_rows = [[p1_ref[h, pl.ds(a0 + al, 1), :] for h in range(N_RETR_HEADS)] for al in range(n_a)]
    xn = xn_ref[...]
    a_per = MXU_DEPTH // N_KEYS
    groups = range(0, n_a, a_per)
    hts = [lax.dot_general(u_ref[g0 * N_KEYS:g0 * N_KEYS + MXU_DEPTH, :], xn, NT_DIMS,
                           preferred_element_type=F32) for g0 in groups]
    for ht, g0 in zip(hts, groups):
        blocks = []
        for al in range(g0, g0 + a_per):
            row = []
            for c in range(0, xn_ref.shape[0], N_KEYS):
                tok = slice(c, c + N_KEYS)
                g = None
                for h in range(N_RETR_HEADS):
                    s = s1_rows[al][h][:, tok] + s2_ref[h, :, tok]
                    w = p1_rows[al][h][:, tok] * p2_ref[h, :, tok]
                    hit = jnp.where(s >= tau_ref[h, 0:1, tok], w, 0.0)
                    g = hit if g is None else g + hit
                r0 = (al - g0) * N_KEYS
                row.append(g * _gelu(ht[r0:r0 + N_KEYS, tok]))
            blocks.append(jnp.concatenate(row, axis=1))
        wt = jnp.concatenate(blocks, axis=0)
        o_ref[...] += jnp.dot(wt.T.astype(BF16), v_ref[g0 * N_KEYS:g0 * N_KEYS + MXU_DEPTH, :],
                              preferred_element_type=F32)


def _peer_dense(xn, x_res, score, peer_u, peer_v, l, *, tm, te):
    s1t, p1t, s2t, p2t, tau = score
    t = xn.shape[0]
    n_exp = peer_u.shape[1]
    n_a = te // N_KEYS
    a_blk = max(n_a, 8)
    nh = N_RETR_HEADS
    assert te % MXU_DEPTH == 0
    once = pl.Buffered(1)
    return pl.pallas_call(
        functools.partial(_peer_dense_kernel, n_a=n_a, a_blk=a_blk),
        grid=(t // tm, n_exp // te),
        in_specs=[
            pl.BlockSpec((tm, D_MODEL), lambda i, e: (i, 0), pipeline_mode=once),
            pl.BlockSpec((None, te, D_MODEL), lambda i, e: (l, e, 0)),
            pl.BlockSpec((None, te, D_MODEL), lambda i, e: (l, e, 0)),
            pl.BlockSpec((nh, a_blk, tm), lambda i, e: (0, e * n_a // a_blk, i)),
            pl.BlockSpec((nh, a_blk, tm), lambda i, e: (0, e * n_a // a_blk, i)),
            pl.BlockSpec((nh, N_KEYS, tm), lambda i, e: (0, 0, i), pipeline_mode=once),
            pl.BlockSpec((nh, N_KEYS, tm), lambda i, e: (0, 0, i), pipeline_mode=once),
            pl.BlockSpec((nh, 8, tm), lambda i, e: (0, 0, i)),
            pl.BlockSpec((tm, D_MODEL), lambda i, e: (i, 0), pipeline_mode=once),
        ],
        out_specs=pl.BlockSpec((tm, D_MODEL), lambda i, e: (i, 0)),
        out_shape=jax.ShapeDtypeStruct((t, D_MODEL), F32),
        compiler_params=_cparams("parallel", "arbitrary"),
        name="peer_dense",
    )(xn, peer_u, peer_v, s1t, p1t, s2t, p2t, tau, x_res)


def kernel(x_prompt, x_sample, mem_prompt, cache_k, cache_v, cache_mem_k, cache_mem_v, state_ssm_re, state_ssm_im, state_pool, page_table, norm_mix, w_in, ssm_a_re, ssm_a_im, ssm_log_dt, ssm_b_re, ssm_b_im, ssm_c_re, ssm_c_im, ssm_d, ssm_w_glu, pool_w, pool_scale, att_lq1, att_lk1, att_lq2, att_lk2, att_subln, w_up_ssm, w_up_pool, w_up_att, w_out, norm_cross, norm_mem, w_mem_q, w_mem_k, w_mem_v, w_mem_o, norm_ffn, peer_w_query, peer_keys1, peer_keys2, peer_u, peer_v, norm_final):
    n_p, seq, d = x_prompt.shape
    n_s, dec_seq, _ = x_sample.shape
    depth = w_in.shape[0]
    tp = n_p * seq
    ts = n_s * dec_seq
    past_len = page_table.shape[1] * cache_k.shape[2]

    tm = 768
    tm_mm = 384
    tc_p, tc_s = 256, dec_seq

    x = jnp.concatenate([x_prompt.reshape(tp, d), x_sample.reshape(ts, d)], axis=0)
    mem_rows = mem_prompt.reshape(n_p * N_MEM, d)
    cmk = cache_mem_k.reshape(depth, n_s, N_MEM, d)
    cmv = cache_mem_v.reshape(depth, n_s, N_MEM, d)
    zeros_h = jnp.zeros((n_p, N_STATE), F32)
    zeros_buf = jnp.zeros((n_p, POOL_CARRY, W_POOL), F32)
    sample_buf = jnp.pad(state_pool, ((0, 0), (0, 0), (POOL_CARRY - POOL_BUF, 0), (0, 0)))
    (w_in, w_up_ssm, w_up_pool, w_up_att, w_out, w_mem_q, w_mem_k, w_mem_v, w_mem_o,
     peer_w_query, peer_u, peer_v) = (
        w.astype(BF16) for w in (w_in, w_up_ssm, w_up_pool, w_up_att, w_out, w_mem_q, w_mem_k,
                                 w_mem_v, w_mem_o, peer_w_query, peer_u, peer_v))

    outs = {k: [] for k in ("kp", "vp", "mkp", "mvp", "hrp", "hip", "bp", "ks", "vs", "hrs", "his", "bs")}
    for l in range(depth):
        a_mat = _in_proj(x, norm_mix[l], w_in, l, COL_GATE, tm=tm, tn=1024)
        k_all = a_mat[:, COL_K:COL_V]
        v_all = a_mat[:, COL_V:COL_GATE]
        outs["kp"].append(k_all[:tp].reshape(n_p, seq, N_HEADS_ATT, W_HEAD))
        outs["vp"].append(v_all[:tp].reshape(n_p, seq, N_HEADS_ATT, W_HEAD))
        outs["ks"].append(k_all[tp:].reshape(n_s, dec_seq, N_HEADS_ATT, W_HEAD))
        outs["vs"].append(v_all[tp:].reshape(n_s, dec_seq, N_HEADS_ATT, W_HEAD))

        s5_tab = _s5_tables(ssm_a_re[l], ssm_a_im[l], ssm_log_dt[l], ssm_b_re[l], ssm_b_im[l],
                            ssm_c_re[l], ssm_c_im[l])
        z, hrp, hip = _s5(a_mat, 0, n_p, seq, s5_tab, ssm_d[l],
                          zeros_h, zeros_h, ssm_w_glu, l, tc=tc_p)
        z, hrs, his = _s5(a_mat, tp, n_s, dec_seq, s5_tab, ssm_d[l],
                          state_ssm_re[l].reshape(n_s, N_STATE), state_ssm_im[l].reshape(n_s, N_STATE),
                          ssm_w_glu, l, tc=tc_s, dst=z)
        outs["hrp"].append(hrp); outs["hip"].append(hip)
        outs["hrs"].append(hrs); outs["his"].append(his)

        yb, bp = _pool(a_mat, 0, n_p, seq, zeros_buf, pool_w, pool_scale[l], l, 0, tc=tc_p)
        yb, bs = _pool(a_mat, tp, n_s, dec_seq, sample_buf[l], pool_w, pool_scale[l], l, past_len,
                       tc=tc_s, dst=yb)
        outs["bp"].append(bp); outs["bs"].append(bs)

        lam_init = 0.8 - 0.6 * math.exp(-0.3 * l)
        lam = (jnp.exp(jnp.sum(att_lq1[l] * att_lk1[l])) - jnp.exp(jnp.sum(att_lq2[l] * att_lk2[l])) + lam_init)
        lam_row = jnp.full((8, W_HEAD), lam, F32)
        yc = _attn_prompt(a_mat, n_p, seq, lam_row, att_subln[l], lam_init, tq=512, hb=2)
        yc = _attn_sample(a_mat, tp, n_s, dec_seq, cache_k, cache_v, page_table, lam_row,
                          att_subln[l], lam_init, l, yc, pages_per_step=PAGE_GROUP)

        merged = _gate_merge(x, norm_mix[l], z, yb, yc, w_in, w_up_ssm, w_up_pool, w_up_att, l,
                             tm=tm, tn=512)
        x = _matmul(merged, w_out, l, res=x, tm=tm_mm, name="w_out")

        mk_p = _matmul(mem_rows, w_mem_k, l, gain=norm_mem[l], tm=N_MEM, name="mem_k")
        mv_p = _matmul(mem_rows, w_mem_v, l, gain=norm_mem[l], tm=N_MEM, name="mem_v")
        outs["mkp"].append(mk_p.reshape(n_p, N_MEM, N_HEADS_MEM, HEAD_DIM_MEM))
        outs["mvp"].append(mv_p.reshape(n_p, N_MEM, N_HEADS_MEM, HEAD_DIM_MEM))
        qm = _matmul(x, w_mem_q, l, gain=norm_cross[l], tm=tm_mm, name="mem_q")
        ca = _cross_attn(qm, 0, n_p, seq, mk_p.reshape(n_p, N_MEM, d), mv_p.reshape(n_p, N_MEM, d),
                         lambda n: (n, 0, 0), tq=512)
        ca = _cross_attn(qm, tp, n_s, dec_seq, cmk, cmv, lambda n: (l, n, 0, 0), tq=dec_seq, dst=ca)
        x = _matmul(ca, w_mem_o, l, res=x, tm=tm_mm, name="mem_o")

        xn, *score = _peer_score(x, norm_ffn[l], peer_w_query, peer_keys1, peer_keys2, l, tm=tm)
        x = _peer_dense(xn, x, score, peer_u, peer_v, l, tm=tm, te=512)

    y_p, y_s = _final_norm(x, norm_final, tp, tm=ts)
    st = lambda k: jnp.stack(outs[k])
    return (y_p.reshape(n_p, seq, d), y_s.reshape(n_s, dec_seq, d),
            st("kp"), st("vp"), st("mkp"), st("mvp"), st("hrp"), st("hip"), st("bp"),
            st("ks"), st("vs"), st("hrs"), st("his"), st("bs"))
```

```python
import functools
import math

import jax
import jax.numpy as jnp
from jax import lax
from jax.experimental import pallas as pl
from jax.experimental.pallas import tpu as pltpu

F32 = jnp.float32
BF16 = jnp.bfloat16
EPS = 1e-6
NEG_INF = -1e30
LOG2E = math.log2(math.e)

D_MODEL = 2048
W_SSM = 512
SSM_GROUP = 16
N_SSM_GROUPS = 32
SSM_STATE = 64
N_STATE = N_SSM_GROUPS * SSM_STATE
W_POOL = 512
POOL_WINDOWS = (2, 4, 8, 16)
POOL_GROUP = 128
POOL_BUF = 15
POOL_CARRY = 16
SCAN_GROUP = 8
N_HEADS_ATT = 8
HEAD_DIM_ATT = 64
W_HEAD = 2 * HEAD_DIM_ATT
W_ATT = N_HEADS_ATT * W_HEAD
N_MEM = 256
N_HEADS_MEM = 4
HEAD_DIM_MEM = 512
N_KEYS = 128
N_RETR_HEADS = 8
D_QUERY = 256
TOPK = 16

COL_SSM = 0
COL_POOL = W_SSM
COL_Q = W_SSM + W_POOL
COL_K = COL_Q + W_ATT
COL_V = COL_K + W_ATT
COL_GATE = COL_V + W_ATT
IN_COLS = COL_GATE + 3 * D_MODEL

VMEM_LIMIT_BYTES = 56 * 1024 * 1024
MXU_DEPTH = 256
PAGE_GROUP = 8
NT_DIMS = (((1,), (1,)), ((), ()))


def _cparams(*sem):
    return pltpu.CompilerParams(dimension_semantics=sem, vmem_limit_bytes=VMEM_LIMIT_BYTES)


def _bdot(a, b):
    return jnp.dot(a.astype(BF16), b.astype(BF16), preferred_element_type=F32)


def _bdot_nt(a, b):
    return lax.dot_general(a.astype(BF16), b.astype(BF16), NT_DIMS, preferred_element_type=F32)


def _rms(x, g):
    return x * lax.rsqrt(jnp.mean(x * x, axis=-1, keepdims=True) + EPS) * g


def _without_ref(fn, idx):
    def wrapped(*refs):
        return fn(*refs[:idx], *refs[idx + 1:])
    return wrapped


_DST_SPEC = pl.BlockSpec(memory_space=pl.ANY)


def _gelu(x):
    return 0.5 * x * (1.0 + lax.erf(x * (2.0 ** -0.5)))


def _in_proj_kernel(x_ref, g_ref, w_ref, o_ref, xn_ref):
    @pl.when(pl.program_id(1) == 0)
    def _():
        xn_ref[...] = _rms(x_ref[...], g_ref[...]).astype(BF16)

    o_ref[...] = jnp.dot(xn_ref[...], w_ref[...], preferred_element_type=F32)


def _in_proj(x, g, w_in, l, n_cols, *, tm, tn):
    t = x.shape[0]
    return pl.pallas_call(
        _in_proj_kernel,
        grid=(t // tm, n_cols // tn),
        in_specs=[
            pl.BlockSpec((tm, D_MODEL), lambda i, j: (i, 0)),
            pl.BlockSpec((1, D_MODEL), lambda i, j: (0, 0)),
            pl.BlockSpec((None, D_MODEL, tn), lambda i, j: (l, 0, j)),
        ],
        out_specs=pl.BlockSpec((tm, tn), lambda i, j: (i, j)),
        out_shape=jax.ShapeDtypeStruct((t, n_cols), F32),
        scratch_shapes=[pltpu.VMEM((tm, D_MODEL), BF16)],
        compiler_params=_cparams("parallel", "arbitrary"),
        name="in_proj",
    )(x, g.reshape(1, D_MODEL), w_in)


def _mm_kernel(*refs, has_norm, has_res):
    x_ref = refs[0]
    pos = 1
    g_ref = None
    if has_norm:
        g_ref = refs[pos]
        pos += 1
    w_ref = refs[pos]
    pos += 1
    r_ref = None
    if has_res:
        r_ref = refs[pos]
        pos += 1
    o_ref = refs[pos]
    x = x_ref[...]
    if has_norm:
        x = _rms(x, g_ref[...])
    res = jnp.dot(x.astype(BF16), w_ref[...], preferred_element_type=F32)
    if has_res:
        res = res + r_ref[...]
    o_ref[...] = res


def _matmul(x, w, l, *, gain=None, res=None, tm, name):
    t, k = x.shape
    n = w.shape[2]
    has_norm = gain is not None
    has_res = res is not None
    in_specs = [pl.BlockSpec((tm, k), lambda i: (i, 0))]
    args = [x]
    if has_norm:
        in_specs.append(pl.BlockSpec((1, k), lambda i: (0, 0)))
        args.append(gain.reshape(1, k))
    in_specs.append(pl.BlockSpec((None, k, n), lambda i: (l, 0, 0), pipeline_mode=pl.Buffered(1)))
    args.append(w)
    if has_res:
        in_specs.append(pl.BlockSpec((tm, n), lambda i: (i, 0)))
        args.append(res)
    return pl.pallas_call(
        functools.partial(_mm_kernel, has_norm=has_norm, has_res=has_res),
        grid=(t // tm,),
        in_specs=in_specs,
        out_specs=pl.BlockSpec((tm, n), lambda i: (i, 0)),
        out_shape=jax.ShapeDtypeStruct((t, n), F32),
        compiler_params=_cparams("parallel"),
        name=name,
    )(*args)


def _gate_merge_kernel(x_ref, g_ref, z_ref, yb_ref, yc_ref, wg0_ref, wg1_ref, wg2_ref,
                       wa_ref, wb_ref, wc_ref, o_ref, xn_ref):
    @pl.when(pl.program_id(1) == 0)
    def _():
        xn_ref[...] = _rms(x_ref[...], g_ref[...]).astype(BF16)

    xn = xn_ref[...]
    acc = None
    for y_ref, wg_ref, wu_ref in ((z_ref, wg0_ref, wa_ref), (yb_ref, wg1_ref, wb_ref),
                                  (yc_ref, wg2_ref, wc_ref)):
        gate = jax.nn.sigmoid(jnp.dot(xn, wg_ref[...], preferred_element_type=F32))
        term = gate * _bdot(y_ref[...], wu_ref[...])
        acc = term if acc is None else acc + term
    o_ref[...] = acc


def _gate_merge(x, g, z, yb, yc, w_in, w_up_ssm, w_up_pool, w_up_att, l, *, tm, tn):
    t = x.shape[0]
    nb = D_MODEL // tn
    gb0 = COL_GATE // tn
    gate_spec = lambda b: pl.BlockSpec((None, D_MODEL, tn), lambda i, j: (l, 0, gb0 + b * nb + j))
    return pl.pallas_call(
        _gate_merge_kernel,
        grid=(t // tm, nb),
        in_specs=[
            pl.BlockSpec((tm, D_MODEL), lambda i, j: (i, 0)),
            pl.BlockSpec((1, D_MODEL), lambda i, j: (0, 0)),
            pl.BlockSpec((tm, W_SSM), lambda i, j: (i, 0)),
            pl.BlockSpec((tm, W_POOL), lambda i, j: (i, 0)),
            pl.BlockSpec((tm, W_ATT), lambda i, j: (i, 0)),
            gate_spec(0), gate_spec(1), gate_spec(2),
            pl.BlockSpec((None, W_SSM, tn), lambda i, j: (l, 0, j)),
            pl.BlockSpec((None, W_POOL, tn), lambda i, j: (l, 0, j)),
            pl.BlockSpec((None, W_ATT, tn), lambda i, j: (l, 0, j)),
        ],
        out_specs=pl.BlockSpec((tm, tn), lambda i, j: (i, j)),
        out_shape=jax.ShapeDtypeStruct((t, D_MODEL), F32),
        scratch_shapes=[pltpu.VMEM((tm, D_MODEL), BF16)],
        compiler_params=_cparams("parallel", "arbitrary"),
        name="gate_merge",
    )(x, g.reshape(1, D_MODEL), z, yb, yc, w_in, w_in, w_in, w_up_ssm, w_up_pool, w_up_att)


def _final_norm_kernel(x_ref, g_ref, yp_ref, ys_ref, *, n_prompt_tiles):
    y = _rms(x_ref[...], g_ref[...])

    @pl.when(pl.program_id(0) < n_prompt_tiles)
    def _():
        yp_ref[...] = y

    @pl.when(pl.program_id(0) >= n_prompt_tiles)
    def _():
        ys_ref[...] = y


def _final_norm(x, g, tp, *, tm):
    t = x.shape[0]
    assert tp % tm == 0 and (t - tp) % tm == 0
    npt = tp // tm
    return pl.pallas_call(
        functools.partial(_final_norm_kernel, n_prompt_tiles=npt),
        grid=(t // tm,),
        in_specs=[pl.BlockSpec((tm, D_MODEL), lambda i: (i, 0)),
                  pl.BlockSpec((1, D_MODEL), lambda i: (0, 0))],
        out_specs=[pl.BlockSpec((tm, D_MODEL), lambda i: (jnp.minimum(i, npt - 1), 0)),
                   pl.BlockSpec((tm, D_MODEL), lambda i: (jnp.maximum(i - npt, 0), 0))],
        out_shape=[jax.ShapeDtypeStruct((tp, D_MODEL), F32),
                   jax.ShapeDtypeStruct((t - tp, D_MODEL), F32)],
        compiler_params=_cparams("arbitrary"),
        name="final_norm",
    )(x, g.reshape(1, D_MODEL))


def _s5_tables(a_re, a_im, log_dt, b_re, b_im, c_re, c_im):
    g_, p_ = N_SSM_GROUPS, SSM_STATE
    dt = jnp.exp(log_dt)[:, None]
    mag = jnp.exp(a_re * dt)
    lb_re = mag * jnp.cos(a_im * dt)
    lb_im = mag * jnp.sin(a_im * dt)
    den = a_re * a_re + a_im * a_im
    nr = lb_re - 1.0
    f_re = (nr * a_re + lb_im * a_im) / den
    f_im = (lb_im * a_re - nr * a_im) / den
    bb_re = f_re[..., None] * b_re - f_im[..., None] * b_im
    bb_im = f_re[..., None] * b_im + f_im[..., None] * b_re
    eye = jnp.eye(g_, dtype=F32)
    bd_re = jnp.einsum('gpc,gh->gchp', bb_re, eye).reshape(W_SSM, N_STATE)
    bd_im = jnp.einsum('gpc,gh->gchp', bb_im, eye).reshape(W_SSM, N_STATE)
    bmat = jnp.concatenate([bd_re, bd_im], axis=1).astype(BF16)
    cd_re = jnp.einsum('gcp,gh->gphc', c_re, eye).reshape(N_STATE, W_SSM).astype(BF16)
    cd_im = jnp.einsum('gcp,gh->gphc', -c_im, eye).reshape(N_STATE, W_SSM).astype(BF16)
    lr = lb_re.reshape(1, N_STATE)
    li = lb_im.reshape(1, N_STATE)
    nsteps = int(math.log2(SCAN_GROUP))
    sq_re, sq_im = [lr], [li]
    for _ in range(nsteps):
        r, i = sq_re[-1], sq_im[-1]
        sq_re.append(r * r - i * i)
        sq_im.append(2.0 * r * i)
    pt_re, pt_im = lr, li
    for s in range(nsteps):
        r, i = sq_re[s], sq_im[s]
        pt_re, pt_im = (jnp.concatenate([pt_re, pt_re * r - pt_im * i], axis=0),
                        jnp.concatenate([pt_im, pt_re * i + pt_im * r], axis=0))
    pw_re = jnp.concatenate(sq_re[:nsteps], axis=0)
    pw_im = jnp.concatenate(sq_im[:nsteps], axis=0)
    return bmat, cd_re, cd_im, pw_re, pw_im, pt_re, pt_im


def _s5_kernel(u_ref, b_ref, cre_ref, cim_ref, d_ref, pwr_ref, pwi_ref, ptr_ref, pti_ref,
               h0r_ref, h0i_ref, wglu_ref, z_ref, hr_ref, hi_ref, cr_ref, ci_ref, *, tc, nsteps):
    c = pl.program_id(1)

    @pl.when(c == 0)
    def _():
        cr_ref[...] = h0r_ref[...]
        ci_ref[...] = h0i_ref[...]

    u = u_ref[...]
    bu = jnp.dot(u.astype(BF16), b_ref[...], preferred_element_type=F32)
    re = bu[:, :N_STATE]
    im = bu[:, N_STATE:]
    row = lax.broadcasted_iota(jnp.int32, (tc, 1), 0) % SCAN_GROUP
    for s in range(nsteps):
        k = 1 << s
        keep = row >= k
        sre = jnp.where(keep, pltpu.roll(re, k, 0), 0.0)
        sim = jnp.where(keep, pltpu.roll(im, k, 0), 0.0)
        ar = pwr_ref[s:s + 1, :]
        ai = pwi_ref[s:s + 1, :]
        re, im = re + (ar * sre - ai * sim), im + (ar * sim + ai * sre)
    hr = cr_ref[...]
    hi = ci_ref[...]
    pr = ptr_ref[...]
    pi = pti_ref[...]
    res, ims = [], []
    for j in range(0, tc, SCAN_GROUP):
        gr = re[j:j + SCAN_GROUP] + (pr * hr - pi * hi)
        gi = im[j:j + SCAN_GROUP] + (pr * hi + pi * hr)
        hr = gr[SCAN_GROUP - 1:SCAN_GROUP, :]
        hi = gi[SCAN_GROUP - 1:SCAN_GROUP, :]
        res.append(gr)
        ims.append(gi)
    re = jnp.concatenate(res, axis=0)
    im = jnp.concatenate(ims, axis=0)
    cr_ref[...] = hr
    ci_ref[...] = hi
    y = (jnp.dot(re.astype(BF16), cre_ref[...], preferred_element_type=F32)
         + jnp.dot(im.astype(BF16), cim_ref[...], preferred_element_type=F32)
         + d_ref[...] * u)
    z = _gelu(y)
    z_ref[...] = z * jax.nn.sigmoid(_bdot(z, wglu_ref[...]))

    @pl.when(c == pl.num_programs(1) - 1)
    def _():
        hr_ref[...] = re[tc - 1:tc, :]
        hi_ref[...] = im[tc - 1:tc, :]


def _s5(a_mat, row0, n_seq, seq_len, tables, d_skip, h0_re, h0_im, w_glu, l, *, tc, dst=None):
    bmat, cd_re, cd_im, pw_re, pw_im, pt_re, pt_im = tables
    nsteps = pw_re.shape[0]
    assert tc % SCAN_GROUP == 0
    nc = seq_len // tc
    rb0 = row0 // tc
    const = lambda shape: pl.BlockSpec(shape, lambda n, c: tuple(0 for _ in shape))
    kern = functools.partial(_s5_kernel, tc=tc, nsteps=nsteps)
    n_in = 12
    z, hr, hi = pl.pallas_call(
        kern if dst is None else _without_ref(kern, n_in),
        grid=(n_seq, nc),
        input_output_aliases={} if dst is None else {n_in: 0},
        in_specs=[
            pl.BlockSpec((tc, W_SSM), lambda n, c: (rb0 + n * nc + c, COL_SSM // W_SSM)),
            const((W_SSM, 2 * N_STATE)),
            const((N_STATE, W_SSM)),
            const((N_STATE, W_SSM)),
            const((1, W_SSM)),
            const(pw_re.shape),
            const(pw_im.shape),
            const((SCAN_GROUP, N_STATE)),
            const((SCAN_GROUP, N_STATE)),
            pl.BlockSpec((None, 1, N_STATE), lambda n, c: (n, 0, 0)),
            pl.BlockSpec((None, 1, N_STATE), lambda n, c: (n, 0, 0)),
            pl.BlockSpec((None, W_SSM, W_SSM), lambda n, c: (l, 0, 0)),
        ] + ([] if dst is None else [_DST_SPEC]),
        out_specs=[
            pl.BlockSpec((tc, W_SSM), lambda n, c: (rb0 + n * nc + c, 0)),
            pl.BlockSpec((None, 1, N_STATE), lambda n, c: (n, 0, 0)),
            pl.BlockSpec((None, 1, N_STATE), lambda n, c: (n, 0, 0)),
        ],
        out_shape=[jax.ShapeDtypeStruct((a_mat.shape[0], W_SSM), F32),
                   jax.ShapeDtypeStruct((n_seq, 1, N_STATE), F32),
                   jax.ShapeDtypeStruct((n_seq, 1, N_STATE), F32)],
        scratch_shapes=[pltpu.VMEM((1, N_STATE), F32), pltpu.VMEM((1, N_STATE), F32)],
        compiler_params=_cparams("parallel", "arbitrary"),
        name="s5",
    )(a_mat, bmat, cd_re, cd_im, d_skip.reshape(1, W_SSM), pw_re, pw_im, pt_re, pt_im,
      h0_re.reshape(n_seq, 1, N_STATE), h0_im.reshape(n_seq, 1, N_STATE), w_glu,
      *(() if dst is None else (dst,)))
    shp = (n_seq, N_SSM_GROUPS, SSM_STATE)
    return z, hr.reshape(shp), hi.reshape(shp)


def _pool_kernel(u_ref, buf_ref, w_ref, sc_ref, y_ref, nb_ref, xx_ref, *, tc, start_pos):
    c = pl.program_id(1)

    @pl.when(c == 0)
    def _():
        xx_ref[0:POOL_CARRY, :] = buf_ref[...]

    u = u_ref[...]
    xx_ref[POOL_CARRY:, :] = u
    x = xx_ref[...]
    sums = []
    s = x
    for w in (1, 2, 4, 8):
        s = s + pltpu.roll(s, w, 0)
        sums.append(s)
    t_abs = (start_pos + c * tc + lax.broadcasted_iota(jnp.int32, (tc, 1), 0) + 1).astype(F32)
    outs = []
    for gi, w in enumerate(POOL_WINDOWS):
        sl = slice(gi * POOL_GROUP, (gi + 1) * POOL_GROUP)
        mean = sums[gi][POOL_CARRY:, sl] / jnp.minimum(t_abs, float(w))
        d = mean - u[:, sl]
        outs.append(_bdot(d, w_ref[gi]))
    y_ref[...] = jnp.concatenate(outs, axis=-1) * sc_ref[...]
    carry = x[tc:, :]
    xx_ref[0:POOL_CARRY, :] = carry

    @pl.when(c == pl.num_programs(1) - 1)
    def _():
        nb_ref[...] = carry


def _pool(a_mat, row0, n_seq, seq_len, buf16, pool_w, pool_scale, l, start_pos, *, tc, dst=None):
    nc = seq_len // tc
    rb0 = row0 // tc
    kern = functools.partial(_pool_kernel, tc=tc, start_pos=start_pos)
    n_in = 4
    y, nb = pl.pallas_call(
        kern if dst is None else _without_ref(kern, n_in),
        grid=(n_seq, nc),
        input_output_aliases={} if dst is None else {n_in: 0},
        in_specs=[
            pl.BlockSpec((tc, W_POOL), lambda n, c: (rb0 + n * nc + c, COL_POOL // W_POOL)),
            pl.BlockSpec((None, POOL_CARRY, W_POOL), lambda n, c: (n, 0, 0)),
            pl.BlockSpec((None, len(POOL_WINDOWS), POOL_GROUP, POOL_GROUP), lambda n, c: (l, 0, 0, 0)),
            pl.BlockSpec((1, W_POOL), lambda n, c: (0, 0)),
        ] + ([] if dst is None else [_DST_SPEC]),
        out_specs=[
            pl.BlockSpec((tc, W_POOL), lambda n, c: (rb0 + n * nc + c, 0)),
            pl.BlockSpec((None, POOL_CARRY, W_POOL), lambda n, c: (n, 0, 0)),
        ],
        out_shape=[jax.ShapeDtypeStruct((a_mat.shape[0], W_POOL), F32),
                   jax.ShapeDtypeStruct((n_seq, POOL_CARRY, W_POOL), F32)],
        scratch_shapes=[pltpu.VMEM((POOL_CARRY + tc, W_POOL), F32)],
        compiler_params=_cparams("parallel", "arbitrary"),
        name="pool",
    )(a_mat, buf16, pool_w, pool_scale.reshape(1, W_POOL), *(() if dst is None else (dst,)))
    return y, nb[:, POOL_CARRY - POOL_BUF:, :]


def _subln(o, g, lam_init):
    return _rms(o, g) * (1.0 - lam_init)


def _attn_prompt_kernel(qt_ref, kt_ref, q_ref, k_ref, v_ref, lam_ref, g_ref, o_ref,
                        m_ref, l_ref, acc_ref, *, tq, hb, lam_init):
    p = pl.program_id(2)
    qi = qt_ref[p]
    ki = kt_ref[p]

    @pl.when(ki == 0)
    def _():
        m_ref[...] = jnp.full(m_ref.shape, NEG_INF, F32)
        l_ref[...] = jnp.zeros(l_ref.shape, F32)
        acc_ref[...] = jnp.zeros(acc_ref.shape, F32)

    def accumulate(masked):
        q = (q_ref[...] * (HEAD_DIM_ATT ** -0.5 * LOG2E)).astype(BF16)
        k = k_ref[...].astype(BF16)
        v = v_ref[...].astype(BF16)
        if masked:
            causal = (lax.broadcasted_iota(jnp.int32, (tq, tq), 0)
                      >= lax.broadcasted_iota(jnp.int32, (tq, tq), 1))
        for hh in range(hb):
            vh = v[:, hh * W_HEAD:(hh + 1) * W_HEAD]
            for c in range(2):
                idx = 2 * hh + c
                sl = slice(hh * W_HEAD + c * HEAD_DIM_ATT, hh * W_HEAD + (c + 1) * HEAD_DIM_ATT)
                s = lax.dot_general(q[:, sl], k[:, sl], NT_DIMS, preferred_element_type=F32)
                if masked:
                    s = jnp.where(causal, s, NEG_INF)
                m_prev = m_ref[idx]
                m_new = jnp.maximum(m_prev, jnp.max(s, axis=-1, keepdims=True))
                alpha = jnp.exp2(m_prev - m_new)
                pexp = jnp.exp2(s - jnp.concatenate([m_new] * (tq // W_HEAD), axis=1))
                l_ref[idx] = alpha * l_ref[idx] + jnp.sum(pexp, axis=-1, keepdims=True)
                acc_ref[idx] = alpha * acc_ref[idx] + jnp.dot(pexp.astype(BF16), vh,
                                                              preferred_element_type=F32)
                m_ref[idx] = m_new

    @pl.when(ki < qi)
    def _():
        accumulate(False)

    @pl.when(ki == qi)
    def _():
        accumulate(True)
        lam = lam_ref[0:1, :]
        for hh in range(hb):
            o = (acc_ref[2 * hh] / l_ref[2 * hh]
                 - lam * (acc_ref[2 * hh + 1] / l_ref[2 * hh + 1]))
            o_ref[:, hh * W_HEAD:(hh + 1) * W_HEAD] = _subln(o, g_ref[...], lam_init)


def _attn_prompt(a_mat, n_seq, seq_len, lam_row, subln_g, lam_init, *, tq, hb):
    nq = seq_len // tq
    pairs = [(qi, ki) for qi in range(nq) for ki in range(qi + 1)]
    qt = jnp.asarray([p[0] for p in pairs], jnp.int32)
    kt = jnp.asarray([p[1] for p in pairs], jnp.int32)
    wb = hb * W_HEAD
    cq, ck, cv = COL_Q // wb, COL_K // wb, COL_V // wb
    grid_spec = pltpu.PrefetchScalarGridSpec(
        num_scalar_prefetch=2,
        grid=(n_seq, N_HEADS_ATT // hb, len(pairs)),
        in_specs=[
            pl.BlockSpec((tq, wb), lambda b, h, p, qt, kt: (b * nq + qt[p], cq + h)),
            pl.BlockSpec((tq, wb), lambda b, h, p, qt, kt: (b * nq + kt[p], ck + h)),
            pl.BlockSpec((tq, wb), lambda b, h, p, qt, kt: (b * nq + kt[p], cv + h)),
            pl.BlockSpec((8, W_HEAD), lambda b, h, p, qt, kt: (0, 0)),
            pl.BlockSpec((1, W_HEAD), lambda b, h, p, qt, kt: (0, 0)),
        ],
        out_specs=pl.BlockSpec((tq, wb), lambda b, h, p, qt, kt: (b * nq + qt[p], h)),
        scratch_shapes=[pltpu.VMEM((2 * hb, tq, W_HEAD), F32), pltpu.VMEM((2 * hb, tq, W_HEAD), F32),
                        pltpu.VMEM((2 * hb, tq, W_HEAD), F32)],
    )
    return pl.pallas_call(
        functools.partial(_attn_prompt_kernel, tq=tq, hb=hb, lam_init=lam_init),
        grid_spec=grid_spec,
        out_shape=jax.ShapeDtypeStruct((a_mat.shape[0], W_ATT), F32),
        compiler_params=_cparams("parallel", "parallel", "arbitrary"),
        name="attn_prompt",
    )(qt, kt, a_mat, a_mat, a_mat, lam_row, subln_g.reshape(1, W_HEAD))


def _attn_sample_kernel(pt_ref, q_ref, kn_ref, vn_ref, lam_ref, g_ref, *rest, n_pages, dec_seq, lam_init):
    k_refs = rest[:n_pages]
    v_refs = rest[n_pages:2 * n_pages]
    o_ref, qh_ref, bias_ref, m_ref, l_ref, acc_ref = rest[2 * n_pages:]
    step = pl.program_id(1)
    rph = 2 * dec_seq

    @pl.when(step == 0)
    def _():
        q = q_ref[...] * (HEAD_DIM_ATT ** -0.5)
        row = lax.broadcasted_iota(jnp.int32, (rph, W_HEAD), 0)
        lane = lax.broadcasted_iota(jnp.int32, (rph, W_HEAD), 1)
        keep = lane // HEAD_DIM_ATT == row // dec_seq
        for h in range(N_HEADS_ATT):
            qh = q[:, h * W_HEAD:(h + 1) * W_HEAD]
            qh_ref[h] = jnp.where(keep, jnp.concatenate([qh, qh], axis=0), 0.0).astype(BF16)
        m_ref[...] = jnp.full(m_ref.shape, NEG_INF, F32)
        l_ref[...] = jnp.zeros(l_ref.shape, F32)
        acc_ref[...] = jnp.zeros(acc_ref.shape, F32)
        brow = lax.broadcasted_iota(jnp.int32, bias_ref.shape, 0) // rph
        bcol = lax.broadcasted_iota(jnp.int32, bias_ref.shape, 1) % N_HEADS_ATT
        bias_ref[...] = jnp.where(brow == bcol, 0.0, NEG_INF)

    def update(s, pv_of):
        m_prev = m_ref[...]
        m_new = jnp.maximum(m_prev, jnp.max(s, axis=-1, keepdims=True))
        alpha = jnp.exp(m_prev - m_new)
        pexp = jnp.exp(s - m_new)
        l_ref[...] = alpha * l_ref[...] + jnp.sum(pexp, axis=-1, keepdims=True)
        acc_ref[...] = alpha * acc_ref[...] + pv_of(pexp.astype(BF16))
        m_ref[...] = m_new

    q_all = jnp.concatenate([qh_ref[h] for h in range(N_HEADS_ATT)], axis=0)
    for g0 in range(0, n_pages, PAGE_GROUP):
        k_all = jnp.concatenate([r[...].astype(BF16) for r in k_refs[g0:g0 + PAGE_GROUP]], axis=0)
        v_all = jnp.concatenate([r[...].astype(BF16) for r in v_refs[g0:g0 + PAGE_GROUP]], axis=0)
        s = lax.dot_general(q_all, k_all, NT_DIMS, preferred_element_type=F32)
        s = s + jnp.concatenate([bias_ref[...]] * PAGE_GROUP, axis=1)
        update(s, lambda pb, v_all=v_all: jnp.dot(pb, v_all, preferred_element_type=F32))

    @pl.when(step == pl.num_programs(1) - 1)
    def _():
        pad = jnp.zeros((W_HEAD - dec_seq, W_HEAD), F32)

        def new_rows(ref, h):
            return jnp.concatenate([ref[:, h * W_HEAD:(h + 1) * W_HEAD], pad], axis=0).astype(BF16)

        s2 = jnp.concatenate(
            [lax.dot_general(qh_ref[h], new_rows(kn_ref, h), NT_DIMS, preferred_element_type=F32)
             for h in range(N_HEADS_ATT)], axis=0)
        qi = lax.broadcasted_iota(jnp.int32, s2.shape, 0) % dec_seq
        kj = lax.broadcasted_iota(jnp.int32, s2.shape, 1)
        s2 = jnp.where(qi >= kj, s2, NEG_INF)
        update(s2, lambda pb: jnp.concatenate(
            [jnp.dot(pb[h * rph:(h + 1) * rph], new_rows(vn_ref, h), preferred_element_type=F32)
             for h in range(N_HEADS_ATT)], axis=0))
        o = acc_ref[...] / l_ref[...]
        lam = lam_ref[0:1, :]
        for h in range(N_HEADS_ATT):
            r0 = h * rph
            oh = o[r0:r0 + dec_seq] - lam * o[r0 + dec_seq:r0 + rph]
            o_ref[:, h * W_HEAD:(h + 1) * W_HEAD] = _subln(oh, g_ref[...], lam_init)


def _attn_sample(a_mat, row0, n_seq, dec_seq, cache_k, cache_v, page_table, lam_row, subln_g,
                 lam_init, l, dst, *, pages_per_step):
    depth, n_pool, page, _, _ = cache_k.shape
    ck = cache_k.reshape(depth, n_pool, page * N_HEADS_ATT, W_HEAD)
    cv = cache_v.reshape(depth, n_pool, page * N_HEADS_ATT, W_HEAD)
    n_pt = page_table.shape[1]
    r = pages_per_step
    rb0 = row0 // dec_seq
    cq, ckk, cvv = COL_Q // W_ATT, COL_K // W_ATT, COL_V // W_ATT

    def page_spec(j):
        return pl.BlockSpec((None, None, page * N_HEADS_ATT, W_HEAD),
                            lambda n, s, pt: (l, pt[n, s * r + j], 0, 0))

    grid_spec = pltpu.PrefetchScalarGridSpec(
        num_scalar_prefetch=1,
        grid=(n_seq, n_pt // r),
        in_specs=[
            pl.BlockSpec((dec_seq, W_ATT), lambda n, s, pt: (rb0 + n, cq)),
            pl.BlockSpec((dec_seq, W_ATT), lambda n, s, pt: (rb0 + n, ckk)),
            pl.BlockSpec((dec_seq, W_ATT), lambda n, s, pt: (rb0 + n, cvv)),
            pl.BlockSpec((8, W_HEAD), lambda n, s, pt: (0, 0)),
            pl.BlockSpec((1, W_HEAD), lambda n, s, pt: (0, 0)),
        ] + [page_spec(j) for j in range(r)] + [page_spec(j) for j in range(r)] + [_DST_SPEC],
        out_specs=pl.BlockSpec((dec_seq, W_ATT), lambda n, s, pt: (rb0 + n, 0)),
        scratch_shapes=[pltpu.VMEM((N_HEADS_ATT, 2 * dec_seq, W_HEAD), BF16),
                        pltpu.VMEM((2 * N_HEADS_ATT * dec_seq, page * N_HEADS_ATT), F32),
                        pltpu.VMEM((2 * N_HEADS_ATT * dec_seq, 1), F32),
                        pltpu.VMEM((2 * N_HEADS_ATT * dec_seq, 1), F32),
                        pltpu.VMEM((2 * N_HEADS_ATT * dec_seq, W_HEAD), F32)],
    )
    n_in = 5 + 2 * r
    kern = functools.partial(_attn_sample_kernel, n_pages=r, dec_seq=dec_seq, lam_init=lam_init)
    return pl.pallas_call(
        _without_ref(kern, 1 + n_in),
        grid_spec=grid_spec,
        input_output_aliases={1 + n_in: 0},
        out_shape=jax.ShapeDtypeStruct(dst.shape, F32),
        compiler_params=_cparams("parallel", "arbitrary"),
        name="attn_sample",
    )(page_table, a_mat, a_mat, a_mat, lam_row, subln_g.reshape(1, W_HEAD),
      *([ck] * r), *([cv] * r), dst)


def _cross_kernel(q_ref, k_ref, v_ref, o_ref):
    q = q_ref[...]
    scale = HEAD_DIM_MEM ** -0.5
    for h in range(N_HEADS_MEM):
        sl = slice(h * HEAD_DIM_MEM, (h + 1) * HEAD_DIM_MEM)
        s = _bdot_nt(q[:, sl], k_ref[:, sl]) * scale
        s = s - jnp.max(s, axis=-1, keepdims=True)
        e = jnp.exp(s)
        pr = e / jnp.sum(e, axis=-1, keepdims=True)
        o_ref[:, sl] = _bdot(pr, v_ref[:, sl])


def _cross_attn(qm, row0, n_seq, seq_len, mem_k, mem_v, kv_index, *, tq, dst=None):
    nq = seq_len // tq
    rb0 = row0 // tq
    blk = tuple(None for _ in range(mem_k.ndim - 2)) + (N_MEM, D_MODEL)
    n_in = 3
    return pl.pallas_call(
        _cross_kernel if dst is None else _without_ref(_cross_kernel, n_in),
        grid=(n_seq, nq),
        input_output_aliases={} if dst is None else {n_in: 0},
        in_specs=[
            pl.BlockSpec((tq, D_MODEL), lambda n, i: (rb0 + n * nq + i, 0)),
            pl.BlockSpec(blk, lambda n, i: kv_index(n)),
            pl.BlockSpec(blk, lambda n, i: kv_index(n)),
        ] + ([] if dst is None else [_DST_SPEC]),
        out_specs=pl.BlockSpec((tq, D_MODEL), lambda n, i: (rb0 + n * nq + i, 0)),
        out_shape=jax.ShapeDtypeStruct(qm.shape, F32),
        compiler_params=_cparams("parallel", "arbitrary"),
        name="cross_attn",
    )(qm, mem_k, mem_v, *(() if dst is None else (dst,)))


def _oddeven_sort_pairs(n):
    pairs = []

    def merge(lo, hi, r):
        step = r * 2
        if step < hi - lo:
            merge(lo, hi, step)
            merge(lo + r, hi, step)
            pairs.extend((i, i + r) for i in range(lo + r, hi - r, step))
        else:
            pairs.append((lo, lo + r))

    def sort(lo, hi):
        if hi - lo >= 1:
            mid = lo + (hi - lo) // 2
            sort(lo, mid)
            sort(mid + 1, hi)
            merge(lo, hi, 1)

    sort(0, n - 1)
    return pairs


_SORT16 = _oddeven_sort_pairs(TOPK)


def _cmpx(xs, i, j):
    a, b = xs[i], xs[j]
    if a is None:
        xs[i], xs[j] = b, None
    elif b is not None:
        xs[i], xs[j] = jnp.maximum(a, b), jnp.minimum(a, b)


def _sort16_desc(xs):
    xs = list(xs)
    for i, j in _SORT16:
        _cmpx(xs, i, j)
    return xs


def _merge_top16(a, b):
    c = []
    for i in range(TOPK):
        x, y = a[i], b[TOPK - 1 - i]
        c.append(y if x is None else (x if y is None else jnp.maximum(x, y)))
    stride = TOPK // 2
    while stride:
        for i in range(TOPK):
            if not i & stride:
                _cmpx(c, i, i + stride)
        stride //= 2
    return c


def _top16_rows(st):
    lists = _sort16_desc([st[8 * v:8 * v + 8, :] for v in range(N_KEYS // 8)])
    for shift in (4, 2, 1):
        lists = _merge_top16(lists, [pltpu.roll(x, shift, 0) for x in lists])
    return lists


_CAND_PAIRS = [(i, j) for i in range(TOPK) for j in range(TOPK) if (i + 1) * (j + 1) <= TOPK]


def _peer_score_kernel(x_ref, g_ref, wq_ref, k1_ref, k2_ref,
                       xn_ref, s1_ref, p1_ref, s2_ref, p2_ref, tau_ref):
    h = pl.program_id(1)

    @pl.when(h == 0)
    def _():
        xn_ref[...] = _rms(x_ref[...], g_ref[...]).astype(BF16)

    q = jnp.dot(xn_ref[...], wq_ref[...].astype(BF16), preferred_element_type=F32)
    half = D_QUERY // 2
    s1 = _bdot_nt(k1_ref[...], q[:, :half])
    s2 = _bdot_nt(k2_ref[...], q[:, half:])
    v1 = _top16_rows(s1)
    v2 = _top16_rows(s2)
    cands = [v1[i] + v2[j] for i, j in _CAND_PAIRS]
    cands += [None] * (-len(cands) % TOPK)
    groups = [_sort16_desc(cands[g:g + TOPK]) for g in range(0, len(cands), TOPK)]
    top = groups[0]
    for grp in groups[1:]:
        top = _merge_top16(top, grp)
    z = jnp.exp(top[0] - top[0])
    for t in top[1:]:
        z = z + jnp.exp(t - top[0])
    s1_ref[...] = s1
    s2_ref[...] = s2
    p1_ref[...] = jnp.exp(s1 - v1[0][0:1, :]) / z[0:1, :]
    p2_ref[...] = jnp.exp(s2 - v2[0][0:1, :])
    tau_ref[...] = top[TOPK - 1]


def _peer_score(x, g, w_query, keys1, keys2, l, *, tm):
    t = x.shape[0]
    nh = N_RETR_HEADS
    key_spec = pl.BlockSpec((None, None, N_KEYS, D_QUERY // 2), lambda i, h: (l, h, 0, 0))
    st_spec = pl.BlockSpec((None, N_KEYS, tm), lambda i, h: (h, 0, i))
    st_shape = jax.ShapeDtypeStruct((nh, N_KEYS, t), F32)
    return pl.pallas_call(
        _peer_score_kernel,
        grid=(t // tm, nh),
        in_specs=[
            pl.BlockSpec((tm, D_MODEL), lambda i, h: (i, 0)),
            pl.BlockSpec((1, D_MODEL), lambda i, h: (0, 0)),
            pl.BlockSpec((None, D_MODEL, D_QUERY), lambda i, h: (l, 0, h)),
            key_spec, key_spec,
        ],
        out_specs=[
            pl.BlockSpec((tm, D_MODEL), lambda i, h: (i, 0)),
            st_spec, st_spec, st_spec, st_spec,
            pl.BlockSpec((None, 8, tm), lambda i, h: (h, 0, i)),
        ],
        out_shape=[jax.ShapeDtypeStruct((t, D_MODEL), BF16),
                   st_shape, st_shape, st_shape, st_shape,
                   jax.ShapeDtypeStruct((nh, 8, t), F32)],
        compiler_params=_cparams("parallel", "arbitrary"),
        name="peer_score",
    )(x, g.reshape(1, D_MODEL), w_query, keys1, keys2)


def _peer_dense_kernel(xn_ref, u_ref, v_ref, s1_ref, p1_ref, s2_ref, p2_ref, tau_ref, x_ref,
                       o_ref, *, n_a, a_blk):
    e = pl.program_id(1)

    @pl.when(e == 0)
    def _():
        o_ref[...] = x_ref[...]

    a0 = (e % (a_blk // n_a)) * n_a
    s1_rows = [[s1_ref[h, pl.ds(a0 + al, 1), :] for h in range(N_RETR_HEADS)] for al in range(n_a)]
    p1_rows = [[p1_ref[h, pl.ds(a0 + al, 1), :] for h in range(N_RETR_HEADS)] for al in range(n_a)]
    xn = xn_ref[...]
    a_per = MXU_DEPTH // N_KEYS
    groups = range(0, n_a, a_per)
    hts = [lax.dot_general(u_ref[g0 * N_KEYS:g0 * N_KEYS + MXU_DEPTH, :], xn, NT_DIMS,
                           preferred_element_type=F32) for g0 in groups]
    for ht, g0 in zip(hts, groups):
        blocks = []
        for al in range(g0, g0 + a_per):
            row = []
            for c in range(0, xn_ref.shape[0], N_KEYS):
                tok = slice(c, c + N_KEYS)
                g = None
                for h in range(N_RETR_HEADS):
                    s = s1_rows[al][h][:, tok] + s2_ref[h, :, tok]
                    w = p1_rows[al][h][:, tok] * p2_ref[h, :, tok]
                    hit = jnp.where(s >= tau_ref[h, 0:1, tok], w, 0.0)
                    g = hit if g is None else g + hit
                r0 = (al - g0) * N_KEYS
                row.append(g * _gelu(ht[r0:r0 + N_KEYS, tok]))
            blocks.append(jnp.concatenate(row, axis=1))
        wt = jnp.concatenate(blocks, axis=0)
        o_ref[...] += jnp.dot(wt.T.astype(BF16), v_ref[g0 * N_KEYS:g0 * N_KEYS + MXU_DEPTH, :],
                              preferred_element_type=F32)


def _peer_dense(xn, x_res, score, peer_u, peer_v, l, *, tm, te):
    s1t, p1t, s2t, p2t, tau = score
    t = xn.shape[0]
    n_exp = peer_u.shape[1]
    n_a = te // N_KEYS
    a_blk = max(n_a, 8)
    nh = N_RETR_HEADS
    assert te % MXU_DEPTH == 0
    once = pl.Buffered(1)
    return pl.pallas_call(
        functools.partial(_peer_dense_kernel, n_a=n_a, a_blk=a_blk),
        grid=(t // tm, n_exp // te),
        in_specs=[
            pl.BlockSpec((tm, D_MODEL), lambda i, e: (i, 0), pipeline_mode=once),
            pl.BlockSpec((None, te, D_MODEL), lambda i, e: (l, e, 0)),
            pl.BlockSpec((None, te, D_MODEL), lambda i, e: (l, e, 0)),
            pl.BlockSpec((nh, a_blk, tm), lambda i, e: (0, e * n_a // a_blk, i)),
            pl.BlockSpec((nh, a_blk, tm), lambda i, e: (0, e * n_a // a_blk, i)),
            pl.BlockSpec((nh, N_KEYS, tm), lambda i, e: (0, 0, i), pipeline_mode=once),
            pl.BlockSpec((nh, N_KEYS, tm), lambda i, e: (0, 0, i), pipeline_mode=once),
            pl.BlockSpec((nh, 8, tm), lambda i, e: (0, 0, i)),
            pl.BlockSpec((tm, D_MODEL), lambda i, e: (i, 0), pipeline_mode=once),
        ],
        out_specs=pl.BlockSpec((tm, D_MODEL), lambda i, e: (i, 0)),
        out_shape=jax.ShapeDtypeStruct((t, D_MODEL), F32),
        compiler_params=_cparams("parallel", "arbitrary"),
        name="peer_dense",
    )(xn, peer_u, peer_v, s1t, p1t, s2t, p2t, tau, x_res)


def kernel(x_prompt, x_sample, mem_prompt, cache_k, cache_v, cache_mem_k, cache_mem_v, state_ssm_re, state_ssm_im, state_pool, page_table, norm_mix, w_in, ssm_a_re, ssm_a_im, ssm_log_dt, ssm_b_re, ssm_b_im, ssm_c_re, ssm_c_im, ssm_d, ssm_w_glu, pool_w, pool_scale, att_lq1, att_lk1, att_lq2, att_lk2, att_subln, w_up_ssm, w_up_pool, w_up_att, w_out, norm_cross, norm_mem, w_mem_q, w_mem_k, w_mem_v, w_mem_o, norm_ffn, peer_w_query, peer_keys1, peer_keys2, peer_u, peer_v, norm_final):
    n_p, seq, d = x_prompt.shape
    n_s, dec_seq, _ = x_sample.shape
    depth = w_in.shape[0]
    tp = n_p * seq
    ts = n_s * dec_seq
    past_len = page_table.shape[1] * cache_k.shape[2]

    tm = 768
    tm_mm = 384
    tc_p, tc_s = 256, dec_seq

    x = jnp.concatenate([x_prompt.reshape(tp, d), x_sample.reshape(ts, d)], axis=0)
    mem_rows = mem_prompt.reshape(n_p * N_MEM, d)
    cmk = cache_mem_k.reshape(depth, n_s, N_MEM, d)
    cmv = cache_mem_v.reshape(depth, n_s, N_MEM, d)
    zeros_h = jnp.zeros((n_p, N_STATE), F32)
    zeros_buf = jnp.zeros((n_p, POOL_CARRY, W_POOL), F32)
    sample_buf = jnp.pad(state_pool, ((0, 0), (0, 0), (POOL_CARRY - POOL_BUF, 0), (0, 0)))
    (w_in, w_up_ssm, w_up_pool, w_up_att, w_out, w_mem_q, w_mem_k, w_mem_v, w_mem_o,
     peer_w_query, peer_u, peer_v) = (
        w.astype(BF16) for w in (w_in, w_up_ssm, w_up_pool, w_up_att, w_out, w_mem_q, w_mem_k,
                                 w_mem_v, w_mem_o, peer_w_query, peer_u, peer_v))

    outs = {k: [] for k in ("kp", "vp", "mkp", "mvp", "hrp", "hip", "bp", "ks", "vs", "hrs", "his", "bs")}
    for l in range(depth):
        a_mat = _in_proj(x, norm_mix[l], w_in, l, COL_GATE, tm=tm, tn=1024)
        k_all = a_mat[:, COL_K:COL_V]
        v_all = a_mat[:, COL_V:COL_GATE]
        outs["kp"].append(k_all[:tp].reshape(n_p, seq, N_HEADS_ATT, W_HEAD))
        outs["vp"].append(v_all[:tp].reshape(n_p, seq, N_HEADS_ATT, W_HEAD))
        outs["ks"].append(k_all[tp:].reshape(n_s, dec_seq, N_HEADS_ATT, W_HEAD))
        outs["vs"].append(v_all[tp:].reshape(n_s, dec_seq, N_HEADS_ATT, W_HEAD))

        s5_tab = _s5_tables(ssm_a_re[l], ssm_a_im[l], ssm_log_dt[l], ssm_b_re[l], ssm_b_im[l],
                            ssm_c_re[l], ssm_c_im[l])
        z, hrp, hip = _s5(a_mat, 0, n_p, seq, s5_tab, ssm_d[l],
                          zeros_h, zeros_h, ssm_w_glu, l, tc=tc_p)
        z, hrs, his = _s5(a_mat, tp, n_s, dec_seq, s5_tab, ssm_d[l],
                          state_ssm_re[l].reshape(n_s, N_STATE), state_ssm_im[l].reshape(n_s, N_STATE),
                          ssm_w_glu, l, tc=tc_s, dst=z)
        outs["hrp"].append(hrp); outs["hip"].append(hip)
        outs["hrs"].append(hrs); outs["his"].append(his)

        yb, bp = _pool(a_mat, 0, n_p, seq, zeros_buf, pool_w, pool_scale[l], l, 0, tc=tc_p)
        yb, bs = _pool(a_mat, tp, n_s, dec_seq, sample_buf[l], pool_w, pool_scale[l], l, past_len,
                       tc=tc_s, dst=yb)
        outs["bp"].append(bp); outs["bs"].append(bs)

        lam_init = 0.8 - 0.6 * math.exp(-0.3 * l)
        lam = (jnp.exp(jnp.sum(att_lq1[l] * att_lk1[l])) - jnp.exp(jnp.sum(att_lq2[l] * att_lk2[l])) + lam_init)
        lam_row = jnp.full((8, W_HEAD), lam, F32)
        yc = _attn_prompt(a_mat, n_p, seq, lam_row, att_subln[l], lam_init, tq=1024, hb=1)
        yc = _attn_sample(a_mat, tp, n_s, dec_seq, cache_k, cache_v, page_table, lam_row,
                          att_subln[l], lam_init, l, yc, pages_per_step=PAGE_GROUP)

        merged = _gate_merge(x, norm_mix[l], z, yb, yc, w_in, w_up_ssm, w_up_pool, w_up_att, l,
                             tm=tm, tn=512)
        x = _matmul(merged, w_out, l, res=x, tm=tm_mm, name="w_out")

        mk_p = _matmul(mem_rows, w_mem_k, l, gain=norm_mem[l], tm=N_MEM, name="mem_k")
        mv_p = _matmul(mem_rows, w_mem_v, l, gain=norm_mem[l], tm=N_MEM, name="mem_v")
        outs["mkp"].append(mk_p.reshape(n_p, N_MEM, N_HEADS_MEM, HEAD_DIM_MEM))
        outs["mvp"].append(mv_p.reshape(n_p, N_MEM, N_HEADS_MEM, HEAD_DIM_MEM))
        qm = _matmul(x, w_mem_q, l, gain=norm_cross[l], tm=tm_mm, name="mem_q")
        ca = _cross_attn(qm, 0, n_p, seq, mk_p.reshape(n_p, N_MEM, d), mv_p.reshape(n_p, N_MEM, d),
                         lambda n: (n, 0, 0), tq=512)
        ca = _cross_attn(qm, tp, n_s, dec_seq, cmk, cmv, lambda n: (l, n, 0, 0), tq=dec_seq, dst=ca)
        x = _matmul(ca, w_mem_o, l, res=x, tm=tm_mm, name="mem_o")

        xn, *score = _peer_score(x, norm_ffn[l], peer_w_query, peer_keys1, peer_keys2, l, tm=tm)
        x = _peer_dense(xn, x, score, peer_u, peer_v, l, tm=tm, te=512)

    y_p, y_s = _final_norm(x, norm_final, tp, tm=ts)
    st = lambda k: jnp.stack(outs[k])
    return (y_p.reshape(n_p, seq, d), y_s.reshape(n_s, dec_seq, d),
            st("kp"), st("vp"), st("mkp"), st("mvp"), st("hrp"), st("hip"), st("bp"),
            st("ks"), st("vs"), st("hrs"), st("his"), st("bs"))
```
